```python
import jax, jax.numpy as jnp
from jax import lax
import numpy as np


D_MODEL = 4096
BATCH = 4
SEQ = 4096
DEPTH = 1
DEC_BATCH = 8
DEC_SEQ = 2048
PAST_LEN = 128

M_HEADS = 8
M_DQK = 256
M_DV = 512
M_CHUNK = 64
F_GROUPS = 4
F_GROUP_DIM = 1024
F_WIDTH = F_GROUPS * F_GROUP_DIM
D_FF = 11008
CONV_W = 3
EPS = 1e-6

QK_W = M_HEADS * M_DQK
V_W = M_HEADS * M_DV
OFF_Q = 0
OFF_K = OFF_Q + QK_W
OFF_V = OFF_K + QK_W
OFF_O = OFF_V + V_W
OFF_I = OFF_O + V_W
OFF_F = OFF_I + 2 * M_HEADS
OFF_FR = OFF_F + 2 * M_HEADS
OFF_GM = OFF_FR + F_WIDTH
OFF_GF = OFF_GM + D_MODEL
IN_W = OFF_GF + D_MODEL

kernel_name = 'hybrid_mlstm_fnet_encoder'


def _rmsnorm(x, g):
    xf = x.astype(jnp.float32)
    y = xf * lax.rsqrt(jnp.mean(xf * xf, axis=-1, keepdims=True) + EPS)
    return (y * g.astype(jnp.float32)).astype(x.dtype)


def _mlstm_dir(q, k, v, ig, lf):
    B, H, S, _ = q.shape
    nc = S // M_CHUNK

    def chunks(a):
        a = a.reshape((B, H, nc, M_CHUNK) + a.shape[3:])
        return jnp.moveaxis(a, 2, 0)

    lower = jnp.tril(jnp.ones((M_CHUNK, M_CHUNK), dtype=bool))

    def step(carry, inp):
        C, n, m = carry
        qc, kc, vc, ic, fc = inp
        b = jnp.cumsum(fc, axis=-1)
        dmat = b[..., :, None] - b[..., None, :] + ic[..., None, :]
        dmat = jnp.where(lower, dmat, -jnp.inf)
        m_inter = b + m[..., None]
        m_t = jnp.maximum(m_inter, jnp.max(dmat, axis=-1))
        s = jnp.einsum('bhtd,bhsd->bhts', qc, kc) * jnp.exp(dmat - m_t[..., None])
        sc = jnp.exp(m_inter - m_t)
        num = sc[..., None] * jnp.einsum('bhvd,bhtd->bhtv', C, qc) + jnp.einsum('bhts,bhsv->bhtv', s, vc)
        den = sc * jnp.einsum('bhd,bhtd->bht', n, qc) + jnp.sum(s, axis=-1)
        h = num / jnp.maximum(jnp.abs(den), jnp.exp(-m_t))[..., None]
        g = b[..., -1]
        a = g[..., None] - b + ic
        m_new = jnp.maximum(g + m, jnp.max(a, axis=-1))
        decay = jnp.exp(g + m - m_new)
        wa = jnp.exp(a - m_new[..., None])
        C = decay[..., None, None] * C + jnp.einsum('bhsv,bhsd->bhvd', wa[..., None] * vc, kc)
        n = decay[..., None] * n + jnp.einsum('bhs,bhsd->bhd', wa, kc)
        return (C, n, m_new), h

    init = (jnp.zeros((B, H, M_DV, M_DQK), jnp.float32),
            jnp.zeros((B, H, M_DQK), jnp.float32),
            jnp.zeros((B, H), jnp.float32))
    _, hs = lax.scan(step, init, (chunks(q), chunks(k), chunks(v), chunks(ig), chunks(lf)))
    return jnp.moveaxis(hs, 0, 2).reshape(B, H, S, M_DV)


def _flip(a):
    return jnp.flip(a, axis=2)


def _mixer(xn, w_in, b_ig, b_fg, mh_g, w_mo, w_fo, b_merge, w_out):
    B, S, _ = xn.shape
    f32 = jnp.float32
    p = xn @ w_in

    def heads(a, d):
        return a.reshape(B, S, M_HEADS, d).transpose(0, 2, 1, 3).astype(f32)

    q = heads(p[..., OFF_Q:OFF_K], M_DQK)
    k = heads(p[..., OFF_K:OFF_V], M_DQK) * (M_DQK ** -0.5)
    v = heads(p[..., OFF_V:OFF_O], M_DV)
    ig = (p[..., OFF_I:OFF_F].astype(f32).reshape(B, S, 2, M_HEADS) + b_ig.astype(f32)).transpose(2, 0, 3, 1)
    lf = jax.nn.log_sigmoid(p[..., OFF_F:OFF_FR].astype(f32).reshape(B, S, 2, M_HEADS) + b_fg.astype(f32)).transpose(2, 0, 3, 1)
    h = _mlstm_dir(q, k, v, ig[0], lf[0]) + _flip(_mlstm_dir(_flip(q), _flip(k), _flip(v), _flip(ig[1]), _flip(lf[1])))
    h = h * lax.rsqrt(jnp.mean(h * h, axis=-1, keepdims=True) + EPS)
    h = h.transpose(0, 2, 1, 3).reshape(B, S, V_W) * mh_g.astype(f32)
    h = h * jax.nn.sigmoid(p[..., OFF_O:OFF_I].astype(f32))
    h_m = h.astype(xn.dtype) @ w_mo
    fr = p[..., OFF_FR:OFF_GM].astype(f32).reshape(B, S, F_GROUPS, F_GROUP_DIM).transpose(0, 2, 1, 3)
    fr = jnp.fft.fft2(fr, axes=(-2, -1), norm='ortho').real
    fr = fr.transpose(0, 2, 1, 3).reshape(B, S, F_WIDTH).astype(xn.dtype)
    h_f = fr @ w_fo
    gates = jax.nn.sigmoid(p[..., OFF_GM:IN_W].astype(f32) + b_merge.astype(f32))
    merged = gates[..., :D_MODEL] * h_m.astype(f32) + gates[..., D_MODEL:] * h_f.astype(f32)
    return merged.astype(xn.dtype) @ w_out


def _ffn(xn, w_up, cw, cb, w_down):
    u = xn @ w_up
    u = lax.conv_general_dilated(u, cw[:, None, :], window_strides=(1,), padding=((CONV_W // 2, CONV_W // 2),),
                                 dimension_numbers=('NWC', 'WIO', 'NWC'), feature_group_count=2 * D_FF) + cb
    a = u[..., :D_FF].astype(jnp.float32)
    val = u[..., D_FF:].astype(jnp.float32)
    h = jax.nn.gelu(a, approximate=False) * val
    return h.astype(xn.dtype) @ w_down


def _trunk(x, norm_mix, w_in, b_igate, b_fgate, mh_norm, w_mlstm_out, w_fourier_out, b_merge, w_out,
           norm_ffn, w_up, conv_w, conv_b, w_down, norm_final):
    for l in range(DEPTH):
        x = x + _mixer(_rmsnorm(x, norm_mix[l]), w_in[l], b_igate[l], b_fgate[l], mh_norm[l],
                       w_mlstm_out[l], w_fourier_out[l], b_merge[l], w_out[l])
        x = x + _ffn(_rmsnorm(x, norm_ffn[l]), w_up[l], conv_w[l], conv_b[l], w_down[l])
    return _rmsnorm(x, norm_final)


def setup_inputs(seed: int = 0) -> dict:
    key = jax.random.key(seed)
    ks = jax.random.split(key, 20)
    f32 = jnp.float32

    def nrm(k, shape, scale):
        return jax.random.normal(k, shape, f32) * scale

    b_f = jnp.broadcast_to(jnp.linspace(3.0, 6.0, M_HEADS, dtype=f32), (DEPTH, 2, M_HEADS))
    return {
        'x_prompt': nrm(ks[0], (BATCH, SEQ, D_MODEL), 1.0),
        'x_sample': nrm(ks[1], (DEC_BATCH, DEC_SEQ, D_MODEL), 1.0),
        'norm_mix': 1.0 + nrm(ks[2], (DEPTH, D_MODEL), 0.02),
        'w_in': nrm(ks[3], (DEPTH, D_MODEL, IN_W), D_MODEL ** -0.5),
        'b_igate': nrm(ks[4], (DEPTH, 2, M_HEADS), 0.1),
        'b_fgate': b_f + nrm(ks[5], (DEPTH, 2, M_HEADS), 0.1),
        'mh_norm': 1.0 + nrm(ks[6], (DEPTH, V_W), 0.02),
        'w_mlstm_out': nrm(ks[7], (DEPTH, V_W, D_MODEL), V_W ** -0.5),
        'w_fourier_out': nrm(ks[8], (DEPTH, F_WIDTH, D_MODEL), F_WIDTH ** -0.5),
        'b_merge': nrm(ks[9], (DEPTH, 2 * D_MODEL), 0.02),
        'w_out': nrm(ks[10], (DEPTH, D_MODEL, D_MODEL), D_MODEL ** -0.5),
        'norm_ffn': 1.0 + nrm(ks[11], (DEPTH, D_MODEL), 0.02),
        'w_up': nrm(ks[12], (DEPTH, D_MODEL, 2 * D_FF), D_MODEL ** -0.5),
        'conv_w': nrm(ks[13], (DEPTH, CONV_W, 2 * D_FF), CONV_W ** -0.5),
        'conv_b': nrm(ks[14], (DEPTH, 2 * D_FF), 0.02),
        'w_down': nrm(ks[15], (DEPTH, D_FF, D_MODEL), D_FF ** -0.5),
        'norm_final': 1.0 + nrm(ks[16], (D_MODEL,), 0.02),
    }


def reference(x_prompt, x_sample, norm_mix, w_in, b_igate, b_fgate, mh_norm, w_mlstm_out, w_fourier_out,
              b_merge, w_out, norm_ffn, w_up, conv_w, conv_b, w_down, norm_final):
    y_prompt = _trunk(x_prompt, norm_mix, w_in, b_igate, b_fgate, mh_norm, w_mlstm_out, w_fourier_out,
                      b_merge, w_out, norm_ffn, w_up, conv_w, conv_b, w_down, norm_final)
    y_sample = _trunk(x_sample, norm_mix, w_in, b_igate, b_fgate, mh_norm, w_mlstm_out, w_fourier_out,
                      b_merge, w_out, norm_ffn, w_up, conv_w, conv_b, w_down, norm_final)
    return (y_prompt, y_sample)
```

```python
import functools
import math

import jax
import jax.numpy as jnp
from jax import lax
from jax.experimental import pallas as pl
from jax.experimental.pallas import tpu as pltpu

F32 = jnp.float32
BF16 = jnp.bfloat16

EPS = 1e-6
F_GROUPS = 4
CONV_W = 3
M_CHUNK = 256

V7X_VMEM_BYTES = 64 * 1024 * 1024
VMEM_LIMIT = V7X_VMEM_BYTES - 8 * 1024 * 1024
LANES = 128
BF16_SUBLANES = 16
GATE_ROWS = 128
NORM_SLAB = 128


def _tile(n, pref):
    return pref if n % pref == 0 else n


def _params(*sem):
    return pltpu.CompilerParams(dimension_semantics=sem, vmem_limit_bytes=VMEM_LIMIT)


def _norm_proj_kernel(x_ref, g_ref, w_ref, *rest, with_gates):
    if with_gates:
        wg_ref, o_ref, gt_ref, xn_ref = rest
    else:
        o_ref, xn_ref = rest

    @pl.when(pl.program_id(1) == 0)
    def _():
        slab = min(NORM_SLAB, x_ref.shape[0])

        def norm_slab(r, carry):
            rows = pl.ds(pl.multiple_of(r * slab, slab), slab)
            x = x_ref[rows, :]
            ms = jnp.mean(x * x, axis=-1, keepdims=True)
            xn_ref[rows, :] = (x * lax.rsqrt(ms + EPS) * g_ref[...]).astype(BF16)
            return carry

        lax.fori_loop(0, x_ref.shape[0] // slab, norm_slab, 0)
        if with_gates:
            gt_ref[...] = lax.dot_general(wg_ref[...], xn_ref[...], (((1,), (1,)), ((), ())),
                                          preferred_element_type=F32)

    o_ref[...] = jnp.dot(xn_ref[...], w_ref[...], preferred_element_type=F32).astype(o_ref.dtype)


def _norm_proj(x, gain, w, w_gate_t=None, *, name):
    t, d = x.shape
    n = w.shape[1]
    tm, tn = _tile(t, 1024), _tile(n, 512)
    with_gates = w_gate_t is not None
    in_specs = [
        pl.BlockSpec((tm, d), lambda i, j: (i, 0), pipeline_mode=pl.Buffered(1)),
        pl.BlockSpec((1, d), lambda i, j: (0, 0)),
        pl.BlockSpec((d, tn), lambda i, j: (0, j)),
    ]
    out_shape = [jax.ShapeDtypeStruct((t, n), BF16)]
    out_specs = [pl.BlockSpec((tm, tn), lambda i, j: (i, j))]
    args = [x, gain.reshape(1, d), w]
    if with_gates:
        in_specs.append(pl.BlockSpec((GATE_ROWS, d), lambda i, j: (0, 0)))
        out_shape.append(jax.ShapeDtypeStruct((GATE_ROWS, t), F32))
        out_specs.append(pl.BlockSpec((GATE_ROWS, tm), lambda i, j: (0, i)))
        args.append(w_gate_t)
    out = pl.pallas_call(
        functools.partial(_norm_proj_kernel, with_gates=with_gates),
        grid=(t // tm, n // tn),
        in_specs=in_specs,
        out_specs=out_specs,
        out_shape=out_shape,
        scratch_shapes=[pltpu.VMEM((tm, d), BF16)],
        compiler_params=_params("parallel", "arbitrary"),
        name=name,
    )(*args)
    return out if with_gates else out[0]


def _mlstm_kernel(big_ref, bfg_ref, q_ref, k_ref, v_ref, gi_ref, gf_ref, o_ref, ct_ref, m_ref,
                  *, heads, dqk, dv, chunk):
    d = pl.program_id(0)
    h = pl.program_id(2)
    c = pl.program_id(3)

    @pl.when(c == 0)
    def _():
        ct_ref[...] = jnp.zeros_like(ct_ref)
        m_ref[...] = jnp.zeros_like(m_ref)

    sign = 1 - 2 * d
    gidx = d * heads + h
    i_row = gi_ref[pl.ds(h, 1), :] + big_ref[gidx]
    xf = gf_ref[pl.ds(h, 1), :] + bfg_ref[gidx]
    lf_row = jnp.minimum(xf, 0.0) - jnp.log1p(jnp.exp(-jnp.abs(xf)))

    row = lax.broadcasted_iota(jnp.int32, (chunk, chunk), 0)
    col = lax.broadcasted_iota(jnp.int32, (chunk, chunk), 1)

    hi = lf_row.astype(BF16).astype(F32)
    r1 = lf_row - hi
    mid = r1.astype(BF16).astype(F32)
    lo = r1 - mid
    prow = lax.broadcasted_iota(jnp.int32, (BF16_SUBLANES, chunk), 0)
    parts = jnp.where(prow == 0, hi, jnp.where(prow == 1, mid, jnp.where(prow == 2, lo, 0.0)))
    before = ((row - col) * sign <= 0)
    b3 = jnp.dot(parts.astype(BF16), before.astype(BF16), preferred_element_type=F32)
    b_row = b3[0:1] + b3[1:2] + b3[2:3]
    g_tot = jnp.sum(lf_row, axis=1, keepdims=True)

    r_row = i_row - b_row
    m_prev = m_ref[0:1, 0:1]
    mask = ((col - row) * sign <= 0)
    rm = jnp.where(mask, r_row, -jnp.inf)
    m_col = jnp.maximum(jnp.max(rm, axis=1, keepdims=True), m_prev)
    m_all = jnp.maximum(jnp.max(r_row, axis=1, keepdims=True), m_prev)
    dmat = jnp.exp(rm - m_col)

    q = q_ref[...]
    k = k_ref[...]
    ones_col = (lax.broadcasted_iota(jnp.int32, (chunk, LANES), 1) == 0).astype(BF16)
    v_aug = jnp.concatenate([v_ref[...], ones_col], axis=1)

    kscale = dqk ** -0.5
    qk = lax.dot_general(q, k, (((1,), (1,)), ((), ())), preferred_element_type=F32)
    s = (qk * kscale * dmat).astype(BF16)
    ct = ct_ref[...]
    inter = jnp.dot(q, ct.astype(BF16), preferred_element_type=F32)
    sc = jnp.exp(m_prev - m_col) * kscale
    numden = sc * inter + jnp.dot(s, v_aug, preferred_element_type=F32)
    num = numden[:, :dv]
    den = numden[:, dv:dv + 1]
    b_col = jnp.sum(jnp.where(row == col, b_row, 0.0), axis=1, keepdims=True)
    floor = jnp.exp(-(b_col + m_col))
    o_ref[...] = num * (1.0 / jnp.maximum(jnp.abs(den), floor))

    wa_row = jnp.exp(r_row - m_all)
    decay = jnp.exp(m_prev - m_all)
    ktw = (k.astype(F32).T * wa_row).astype(BF16)
    ct_ref[...] = decay * ct + jnp.dot(ktw, v_aug, preferred_element_type=F32)
    m_ref[...] = jnp.broadcast_to(g_tot + m_all, m_ref.shape)


def _mlstm(p, gates_t, b_ig, b_fg, *, batch, seq, heads, dqk, dv):
    t = batch * seq
    chunk = _tile(seq, M_CHUNK)
    nc = seq // chunk
    koff = heads
    voff = 2 * heads * dqk // dv

    def rows(d, b, c):
        return b * nc + c + d * (nc - 1 - 2 * c)

    smem = pl.BlockSpec(memory_space=pltpu.SMEM)
    return pl.pallas_call(
        functools.partial(_mlstm_kernel, heads=heads, dqk=dqk, dv=dv, chunk=chunk),
        grid=(2, batch, heads, nc),
        in_specs=[
            smem, smem,
            pl.BlockSpec((chunk, dqk), lambda d, b, h, c: (rows(d, b, c), h)),
            pl.BlockSpec((chunk, dqk), lambda d, b, h, c: (rows(d, b, c), koff + h)),
            pl.BlockSpec((chunk, dv), lambda d, b, h, c: (rows(d, b, c), voff + h)),
            pl.BlockSpec((heads, chunk), lambda d, b, h, c: (d, rows(d, b, c))),
            pl.BlockSpec((heads, chunk), lambda d, b, h, c: (2 + d, rows(d, b, c))),
        ],
        out_specs=pl.BlockSpec((None, chunk, dv), lambda d, b, h, c: (d, rows(d, b, c), h)),
        out_shape=jax.ShapeDtypeStruct((2, t, heads * dv), F32),
        scratch_shapes=[pltpu.VMEM((dqk, dv + LANES), F32), pltpu.VMEM((8, LANES), F32)],
        compiler_params=_params("parallel", "parallel", "parallel", "arbitrary"),
        name="mlstm",
    )(b_ig, b_fg, p, p, p, gates_t, gates_t)


def _head_norm_kernel(h_ref, og_ref, g_ref, o_ref):
    hs = h_ref[0] + h_ref[1]
    ms = jnp.mean(hs * hs, axis=-1, keepdims=True)
    y = hs * lax.rsqrt(ms + EPS) * g_ref[...]
    o_ref[...] = (y * jax.nn.sigmoid(og_ref[...].astype(F32))).astype(BF16)


def _head_norm(h2, p, gain, *, heads, dv, ooff):
    _, t, vw = h2.shape
    tm = _tile(t, 1024)
    return pl.pallas_call(
        _head_norm_kernel,
        grid=(t // tm, heads),
        in_specs=[
            pl.BlockSpec((2, tm, dv), lambda i, h: (0, i, h)),
            pl.BlockSpec((tm, dv), lambda i, h: (i, ooff // dv + h)),
            pl.BlockSpec((1, dv), lambda i, h: (0, h)),
        ],
        out_specs=pl.BlockSpec((tm, dv), lambda i, h: (i, h)),
        out_shape=jax.ShapeDtypeStruct((t, vw), BF16),
        compiler_params=_params("parallel", "parallel"),
        name="head_norm",
    )(h2, p, gain.reshape(1, vw))


def _twiddles(n):
    idx = jnp.arange(n, dtype=jnp.int32)
    ang = ((idx[:, None] * idx[None, :]) % n).astype(F32) * (2.0 * math.pi / n)
    return jnp.cos(ang), jnp.sin(ang)


def _dft_chan_kernel(x_ref, w_ref, o_ref):
    o_ref[...] = jnp.dot(x_ref[...], w_ref[...], preferred_element_type=F32).astype(BF16)


def _dft_chan(p, w_cs, *, groups, gdim, froff):
    t = p.shape[0]
    tm = _tile(t, 1024)
    return pl.pallas_call(
        _dft_chan_kernel,
        grid=(t // tm, groups),
        in_specs=[
            pl.BlockSpec((tm, gdim), lambda i, g: (i, froff // gdim + g)),
            pl.BlockSpec((gdim, 2 * gdim), lambda i, g: (0, 0)),
        ],
        out_specs=pl.BlockSpec((tm, 2 * gdim), lambda i, g: (i, g)),
        out_shape=jax.ShapeDtypeStruct((t, 2 * groups * gdim), BF16),
        compiler_params=_params("parallel", "parallel"),
        name="dft_chan",
    )(p, w_cs)


def _dft_pos_kernel(c_ref, s_ref, y1_ref, y2_ref, o_ref, *, scale):
    acc = jnp.dot(c_ref[...], y1_ref[...], preferred_element_type=F32)
    acc = acc + jnp.dot(s_ref[...], y2_ref[...], preferred_element_type=F32)
    o_ref[...] = (acc * scale).astype(BF16)


def _dft_pos(y, cos_s, nsin_s, *, batch, seq, groups, gdim):
    t = batch * seq
    tk = _tile(seq, 512)
    nk = seq // tk
    scale = 1.0 / math.sqrt(seq * gdim)
    return pl.pallas_call(
        functools.partial(_dft_pos_kernel, scale=scale),
        grid=(batch, groups, nk),
        in_specs=[
            pl.BlockSpec((tk, seq), lambda b, g, i: (i, 0)),
            pl.BlockSpec((tk, seq), lambda b, g, i: (i, 0)),
            pl.BlockSpec((seq, gdim), lambda b, g, i: (b, 2 * g), pipeline_mode=pl.Buffered(1)),
            pl.BlockSpec((seq, gdim), lambda b, g, i: (b, 2 * g + 1), pipeline_mode=pl.Buffered(1)),
        ],
        out_specs=pl.BlockSpec((tk, gdim), lambda b, g, i: (b * nk + i, g)),
        out_shape=jax.ShapeDtypeStruct((t, groups * gdim), BF16),
        compiler_params=_params("parallel", "parallel", "arbitrary"),
        name="dft_pos",
    )(cos_s, nsin_s, y, y)


def _merge_kernel(hn_ref, fr_ref, wmo_ref, wfo_ref, gm_ref, gf_ref, bm_ref, bf_ref, o_ref):
    hm = jnp.dot(hn_ref[...], wmo_ref[...], preferred_element_type=F32)
    hf = jnp.dot(fr_ref[...], wfo_ref[...], preferred_element_type=F32)
    g1 = jax.nn.sigmoid(gm_ref[...].astype(F32) + bm_ref[...])
    g2 = jax.nn.sigmoid(gf_ref[...].astype(F32) + bf_ref[...])
    o_ref[...] = (g1 * hm + g2 * hf).astype(BF16)


def _merge(hn, fr, w_mo, w_fo, p, b_merge, *, gmoff):
    t, vw = hn.shape
    fw = fr.shape[1]
    d = w_mo.shape[1]
    tm, tn = _tile(t, 1024), _tile(d, 512)
    nj = d // tn
    bm = b_merge.reshape(1, 2 * d)
    return pl.pallas_call(
        _merge_kernel,
        grid=(t // tm, nj),
        in_specs=[
            pl.BlockSpec((tm, vw), lambda i, j: (i, 0), pipeline_mode=pl.Buffered(1)),
            pl.BlockSpec((tm, fw), lambda i, j: (i, 0), pipeline_mode=pl.Buffered(1)),
            pl.BlockSpec((vw, tn), lambda i, j: (0, j)),
            pl.BlockSpec((fw, tn), lambda i, j: (0, j)),
            pl.BlockSpec((tm, tn), lambda i, j: (i, gmoff // tn + j)),
            pl.BlockSpec((tm, tn), lambda i, j: (i, gmoff // tn + nj + j)),
            pl.BlockSpec((1, tn), lambda i, j: (0, j)),
            pl.BlockSpec((1, tn), lambda i, j: (0, nj + j)),
        ],
        out_specs=pl.BlockSpec((tm, tn), lambda i, j: (i, j)),
        out_shape=jax.ShapeDtypeStruct((t, d), BF16),
        compiler_params=_params("parallel", "arbitrary"),
        name="merge",
    )(hn, fr, w_mo, w_fo, p, p, bm, bm)


def _proj_res_kernel(a_ref, w_ref, r_ref, o_ref, *scratch, nk):
    part = jnp.dot(a_ref[...], w_ref[...], preferred_element_type=F32)
    if nk == 1:
        o_ref[...] = r_ref[...] + part
        return
    acc_ref, = scratch
    k = pl.program_id(2)

    @pl.when(k == 0)
    def _():
        acc_ref[...] = part

    @pl.when(jnp.logical_and(k > 0, k < nk - 1))
    def _():
        acc_ref[...] += part

    @pl.when(k == nk - 1)
    def _():
        o_ref[...] = r_ref[...] + (acc_ref[...] + part)


def _proj_res(a, w, res, *, tk, name):
    t, kdim = a.shape
    n = w.shape[1]
    tm, tn = _tile(t, 1024), _tile(n, 512)
    nk = kdim // tk
    scratch = [pltpu.VMEM((tm, tn), F32)] if nk > 1 else []
    return pl.pallas_call(
        functools.partial(_proj_res_kernel, nk=nk),
        grid=(t // tm, n // tn, nk),
        in_specs=[
            pl.BlockSpec((tm, tk), lambda i, j, k: (i, k)),
            pl.BlockSpec((tk, tn), lambda i, j, k: (k, j)),
            pl.BlockSpec((tm, tn), lambda i, j, k: (i, j)),
        ],
        out_specs=pl.BlockSpec((tm, tn), lambda i, j, k: (i, j)),
        out_shape=jax.ShapeDtypeStruct((t, n), F32),
        scratch_shapes=scratch,
        compiler_params=_params("parallel", "parallel", "arbitrary"),
        name=name,
    )(a, w, res)


def _conv_glu_kernel(a_ref, ap_ref, an_ref, v_ref, vp_ref, vn_ref, cwa_ref, cwv_ref, cba_ref, cbv_ref,
                     o_ref, *, tm, seq):
    r0 = pl.program_id(0) * tm
    at_start = (r0 % seq) == 0
    at_end = ((r0 + tm) % seq) == 0
    rows = lax.broadcasted_iota(jnp.int32, a_ref.shape, 0)

    def conv(x_ref, p_ref, n_ref, cw_ref, cb_ref):
        x = x_ref[...].astype(F32)
        prev = jnp.where(at_start, 0.0, p_ref[BF16_SUBLANES - 1:BF16_SUBLANES, :].astype(F32))
        nxt = jnp.where(at_end, 0.0, n_ref[0:1, :].astype(F32))
        up = jnp.where(rows == 0, prev, pltpu.roll(x, 1, 0))
        dn = jnp.where(rows == tm - 1, nxt, pltpu.roll(x, tm - 1, 0))
        cw = cw_ref[...]
        return cw[0:1] * up + cw[1:2] * x + cw[2:3] * dn + cb_ref[...]

    a = conv(a_ref, ap_ref, an_ref, cwa_ref, cba_ref)
    val = conv(v_ref, vp_ref, vn_ref, cwv_ref, cbv_ref)
    gelu = 0.5 * a * (1.0 + lax.erf(a * math.sqrt(0.5)))
    o_ref[...] = (gelu * val).astype(BF16)


def _conv_glu(u, cw, cb, *, seq):
    t, n2 = u.shape
    ff = n2 // 2
    tm, tc = _tile(seq, 512), _tile(ff, 1024)
    nj = ff // tc
    hb = tm // BF16_SUBLANES
    last_hb = t // BF16_SUBLANES - 1
    cb2 = cb.reshape(1, n2)

    def prev_map(off):
        return lambda i, j: (jnp.maximum(i * hb - 1, 0), off + j)

    def next_map(off):
        return lambda i, j: (jnp.minimum((i + 1) * hb, last_hb), off + j)

    def main_map(off):
        return lambda i, j: (i, off + j)

    in_specs = []
    for off in (0, nj):
        in_specs += [pl.BlockSpec((tm, tc), main_map(off)),
                     pl.BlockSpec((BF16_SUBLANES, tc), prev_map(off)),
                     pl.BlockSpec((BF16_SUBLANES, tc), next_map(off))]
    in_specs += [pl.BlockSpec((CONV_W, tc), lambda i, j: (0, j)),
                 pl.BlockSpec((CONV_W, tc), lambda i, j: (0, nj + j)),
                 pl.BlockSpec((1, tc), lambda i, j: (0, j)),
                 pl.BlockSpec((1, tc), lambda i, j: (0, nj + j))]
    return pl.pallas_call(
        functools.partial(_conv_glu_kernel, tm=tm, seq=seq),
        grid=(t // tm, nj),
        in_specs=in_specs,
        out_specs=pl.BlockSpec((tm, tc), lambda i, j: (i, j)),
        out_shape=jax.ShapeDtypeStruct((t, ff), BF16),
        compiler_params=_params("parallel", "parallel"),
        name="conv_glu",
    )(u, u, u, u, u, u, cw, cw, cb2, cb2)


def _final_norm_kernel(x_ref, g_ref, o_ref):
    x = x_ref[...]
    ms = jnp.mean(x * x, axis=-1, keepdims=True)
    o_ref[...] = x * lax.rsqrt(ms + EPS) * g_ref[...]


def _final_norm(x, gain):
    t, d = x.shape
    tm = _tile(t, 256)
    return pl.pallas_call(
        _final_norm_kernel,
        grid=(t // tm,),
        in_specs=[pl.BlockSpec((tm, d), lambda i: (i, 0)), pl.BlockSpec((1, d), lambda i: (0, 0))],
        out_specs=pl.BlockSpec((tm, d), lambda i: (i, 0)),
        out_shape=jax.ShapeDtypeStruct((t, d), F32),
        compiler_params=_params("parallel"),
        name="final_norm",
    )(x, gain.reshape(1, d))


def _pad_cols(a, n):
    return jnp.pad(a, ((0, 0), (0, n - a.shape[1])))


def _prep_weights(w_in, b_igate, b_fgate, w_mlstm_out, w_fourier_out, w_out, w_up, conv_w, conv_b, w_down,
                  *, heads, vw, fw):
    d, in_w = w_in.shape[1:]
    qk_w = (in_w - 2 * vw - 4 * heads - fw - 2 * d) // 2
    off_i = 2 * qk_w + 2 * vw
    off_fr = off_i + 4 * heads
    w = w_in[0]
    w_main = jnp.concatenate([w[:, :off_i], w[:, off_fr:]], axis=1).astype(BF16)
    w_gate_t = jnp.pad(w[:, off_i:off_fr].T, ((0, GATE_ROWS - 4 * heads), (0, 0))).astype(BF16)
    ff = w_down.shape[1]
    ffp = -(-ff // 1024) * 1024
    w_up_p = jnp.concatenate([_pad_cols(w_up[0][:, :ff], ffp), _pad_cols(w_up[0][:, ff:], ffp)], axis=1).astype(BF16)
    cw_p = jnp.concatenate([_pad_cols(conv_w[0][:, :ff], ffp), _pad_cols(conv_w[0][:, ff:], ffp)], axis=1)
    cb_p = jnp.concatenate([jnp.pad(conv_b[0][:ff], (0, ffp - ff)), jnp.pad(conv_b[0][ff:], (0, ffp - ff))])
    w_down_p = jnp.pad(w_down[0], ((0, ffp - ff), (0, 0))).astype(BF16)
    return dict(
        w_main=w_main, w_gate_t=w_gate_t, qk_w=qk_w,
        b_ig=b_igate[0].reshape(-1), b_fg=b_fgate[0].reshape(-1),
        w_mo=w_mlstm_out[0].astype(BF16), w_fo=w_fourier_out[0].astype(BF16), w_out=w_out[0].astype(BF16),
        w_up=w_up_p, cw=cw_p, cb=cb_p, w_down=w_down_p, ffp=ffp,
    )


def _trunk(x, wts, norm_mix, mh_norm, b_merge, norm_ffn, norm_final, *, heads, groups):
    batch, seq, d = x.shape
    t = batch * seq
    x2 = x.reshape(t, d)
    vw = wts["w_mo"].shape[0]
    fw = wts["w_fo"].shape[0]
    qk_w = wts["qk_w"]
    dqk, dv, gdim = qk_w // heads, vw // heads, fw // groups
    ooff = 2 * qk_w + vw
    froff = ooff + vw
    gmoff = froff + fw

    p, gates_t = _norm_proj(x2, norm_mix[0], wts["w_main"], wts["w_gate_t"], name="in_proj")
    h2 = _mlstm(p, gates_t, wts["b_ig"], wts["b_fg"], batch=batch, seq=seq, heads=heads, dqk=dqk, dv=dv)
    hn = _head_norm(h2, p, mh_norm[0], heads=heads, dv=dv, ooff=ooff)

    cos_c, sin_c = _twiddles(gdim)
    cos_s, sin_s = _twiddles(seq)
    y = _dft_chan(p, jnp.concatenate([cos_c, sin_c], axis=1).astype(BF16), groups=groups, gdim=gdim, froff=froff)
    fr = _dft_pos(y, cos_s.astype(BF16), (-sin_s).astype(BF16), batch=batch, seq=seq, groups=groups, gdim=gdim)

    merged = _merge(hn, fr, wts["w_mo"], wts["w_fo"], p, b_merge[0], gmoff=gmoff)
    x1 = _proj_res(merged, wts["w_out"], x2, tk=d, name="out_proj")

    u = _norm_proj(x1, norm_ffn[0], wts["w_up"], name="up_proj")
    hf = _conv_glu(u, wts["cw"], wts["cb"], seq=seq)
    ffp = wts["ffp"]
    x3 = _proj_res(hf, wts["w_down"], x1, tk=ffp // 4, name="down_proj")
    return _final_norm(x3, norm_final).reshape(batch, seq, d)


def kernel(x_prompt, x_sample, norm_mix, w_in, b_igate, b_fgate, mh_norm, w_mlstm_out, w_fourier_out, b_merge, w_out, norm_ffn, w_up, conv_w, conv_b, w_down, norm_final):
    heads = b_igate.shape[-1]
    vw, fw = w_mlstm_out.shape[1], w_fourier_out.shape[1]
    wts = _prep_weights(w_in, b_igate, b_fgate, w_mlstm_out, w_fourier_out, w_out, w_up, conv_w, conv_b, w_down,
                        heads=heads, vw=vw, fw=fw)
    run = functools.partial(_trunk, wts=wts, norm_mix=norm_mix, mh_norm=mh_norm, b_merge=b_merge,
                            norm_ffn=norm_ffn, norm_final=norm_final, heads=heads, groups=F_GROUPS)
    return (run(x_prompt), run(x_sample))
```

```python
import functools
import math

import jax
import jax.numpy as jnp
from jax import lax
from jax.experimental import pallas as pl
from jax.experimental.pallas import tpu as pltpu

F32 = jnp.float32
BF16 = jnp.bfloat16

EPS = 1e-6
F_GROUPS = 4
CONV_W = 3
M_CHUNK = 256
M_HEADS_PER_STEP = 4

V7X_VMEM_BYTES = 64 * 1024 * 1024
VMEM_LIMIT = V7X_VMEM_BYTES - 8 * 1024 * 1024
LANES = 128
BF16_SUBLANES = 16
GATE_ROWS = 128
NORM_SLAB = 128


def _tile(n, pref):
    return pref if n % pref == 0 else n


def _params(*sem):
    return pltpu.CompilerParams(dimension_semantics=sem, vmem_limit_bytes=VMEM_LIMIT)


def _norm_proj_kernel(x_ref, g_ref, w_ref, *rest, with_gates):
    if with_gates:
        wg_ref, o_ref, gt_ref, xn_ref = rest
    else:
        o_ref, xn_ref = rest

    @pl.when(pl.program_id(1) == 0)
    def _():
        slab = min(NORM_SLAB, x_ref.shape[0])

        def norm_slab(r, carry):
            rows = pl.ds(pl.multiple_of(r * slab, slab), slab)
            x = x_ref[rows, :]
            ms = jnp.mean(x * x, axis=-1, keepdims=True)
            xn_ref[rows, :] = (x * lax.rsqrt(ms + EPS) * g_ref[...]).astype(BF16)
            return carry

        lax.fori_loop(0, x_ref.shape[0] // slab, norm_slab, 0)
        if with_gates:
            gt_ref[...] = lax.dot_general(wg_ref[...], xn_ref[...], (((1,), (1,)), ((), ())),
                                          preferred_element_type=F32)

    o_ref[...] = jnp.dot(xn_ref[...], w_ref[...], preferred_element_type=F32).astype(o_ref.dtype)


def _norm_proj(x, gain, w, w_gate_t=None, *, name):
    t, d = x.shape
    n = w.shape[1]
    tm, tn = _tile(t, 1024), _tile(n, 512)
    with_gates = w_gate_t is not None
    in_specs = [
        pl.BlockSpec((tm, d), lambda i, j: (i, 0), pipeline_mode=pl.Buffered(1)),
        pl.BlockSpec((1, d), lambda i, j: (0, 0)),
        pl.BlockSpec((d, tn), lambda i, j: (0, j)),
    ]
    out_shape = [jax.ShapeDtypeStruct((t, n), BF16)]
    out_specs = [pl.BlockSpec((tm, tn), lambda i, j: (i, j))]
    args = [x, gain.reshape(1, d), w]
    if with_gates:
        in_specs.append(pl.BlockSpec((GATE_ROWS, d), lambda i, j: (0, 0)))
        out_shape.append(jax.ShapeDtypeStruct((GATE_ROWS, t), F32))
        out_specs.append(pl.BlockSpec((GATE_ROWS, tm), lambda i, j: (0, i)))
        args.append(w_gate_t)
    out = pl.pallas_call(
        functools.partial(_norm_proj_kernel, with_gates=with_gates),
        grid=(t // tm, n // tn),
        in_specs=in_specs,
        out_specs=out_specs,
        out_shape=out_shape,
        scratch_shapes=[pltpu.VMEM((tm, d), BF16)],
        compiler_params=_params("parallel", "arbitrary"),
        name=name,
    )(*args)
    return out if with_gates else out[0]


def _mlstm_kernel(big_ref, bfg_ref, q_ref, k_ref, v_ref, gi_ref, gf_ref, o_ref, ct_ref, m_ref,
                  *, heads, hps, dqk, dv, chunk):
    d = pl.program_id(0)
    hg = pl.program_id(2)
    c = pl.program_id(3)

    @pl.when(c == 0)
    def _():
        ct_ref[...] = jnp.zeros_like(ct_ref)
        m_ref[...] = jnp.zeros_like(m_ref)

    sign = 1 - 2 * d
    row = lax.broadcasted_iota(jnp.int32, (chunk, chunk), 0)
    col = lax.broadcasted_iota(jnp.int32, (chunk, chunk), 1)
    before = ((row - col) * sign <= 0).astype(BF16)
    mask = ((col - row) * sign <= 0)
    diag = row == col
    prow = lax.broadcasted_iota(jnp.int32, (BF16_SUBLANES, chunk), 0)
    ones_col = (lax.broadcasted_iota(jnp.int32, (chunk, LANES), 1) == 0).astype(BF16)
    kscale = dqk ** -0.5

    for u in range(hps):
        h = hg * hps + u
        gidx = d * heads + h
        i_row = gi_ref[pl.ds(h, 1), :] + big_ref[gidx]
        xf = gf_ref[pl.ds(h, 1), :] + bfg_ref[gidx]
        lf_row = jnp.minimum(xf, 0.0) - jnp.log1p(jnp.exp(-jnp.abs(xf)))

        hi = lf_row.astype(BF16).astype(F32)
        r1 = lf_row - hi
        mid = r1.astype(BF16).astype(F32)
        lo = r1 - mid
        parts = jnp.where(prow == 0, hi, jnp.where(prow == 1, mid, jnp.where(prow == 2, lo, 0.0)))
        b3 = jnp.dot(parts.astype(BF16), before, preferred_element_type=F32)
        b_row = b3[0:1] + b3[1:2] + b3[2:3]
        g_tot = jnp.sum(lf_row, axis=1, keepdims=True)

        r_row = i_row - b_row
        m_prev = m_ref[u, 0:1, 0:1]
        rm = jnp.where(mask, r_row, -jnp.inf)
        m_col = jnp.maximum(jnp.max(rm, axis=1, keepdims=True), m_prev)
        m_all = jnp.maximum(jnp.max(r_row, axis=1, keepdims=True), m_prev)
        dmat = jnp.exp(rm - m_col)

        q = q_ref[:, u * dqk:(u + 1) * dqk]
        k = k_ref[:, u * dqk:(u + 1) * dqk]
        v_aug = jnp.concatenate([v_ref[:, u * dv:(u + 1) * dv], ones_col], axis=1)

        qk = lax.dot_general(q, k, (((1,), (1,)), ((), ())), preferred_element_type=F32)
        s = (qk * kscale * dmat).astype(BF16)
        ct = ct_ref[u]
        inter = jnp.dot(q, ct.astype(BF16), preferred_element_type=F32)
        sc = jnp.exp(m_prev - m_col) * kscale
        numden = sc * inter + jnp.dot(s, v_aug, preferred_element_type=F32)
        num = numden[:, :dv]
        den = numden[:, dv:dv + 1]
        b_col = jnp.sum(jnp.where(diag, b_row, 0.0), axis=1, keepdims=True)
        floor = jnp.exp(-(b_col + m_col))
        o_ref[:, u * dv:(u + 1) * dv] = num * (1.0 / jnp.maximum(jnp.abs(den), floor))

        wa_row = jnp.exp(r_row - m_all)
        decay = jnp.exp(m_prev - m_all)
        ktw = (k.astype(F32).T * wa_row).astype(BF16)
        ct_ref[u] = decay * ct + jnp.dot(ktw, v_aug, preferred_element_type=F32)
        m_ref[u] = jnp.broadcast_to(g_tot + m_all, m_ref.shape[1:])


def _mlstm(p, gates_t, b_ig, b_fg, *, batch, seq, heads, dqk, dv):
    t = batch * seq
    chunk = _tile(seq, M_CHUNK)
    nc = seq // chunk
    hps = M_HEADS_PER_STEP
    ng = heads // hps
    koff = ng
    voff = 2 * heads * dqk // (hps * dv)

    def rows(d, b, c):
        return b * nc + c + d * (nc - 1 - 2 * c)

    smem = pl.BlockSpec(memory_space=pltpu.SMEM)
    return pl.pallas_call(
        functools.partial(_mlstm_kernel, heads=heads, hps=hps, dqk=dqk, dv=dv, chunk=chunk),
        grid=(2, batch, ng, nc),
        in_specs=[
            smem, smem,
            pl.BlockSpec((chunk, hps * dqk), lambda d, b, g, c: (rows(d, b, c), g)),
            pl.BlockSpec((chunk, hps * dqk), lambda d, b, g, c: (rows(d, b, c), koff + g)),
            pl.BlockSpec((chunk, hps * dv), lambda d, b, g, c: (rows(d, b, c), voff + g)),
            pl.BlockSpec((heads, chunk), lambda d, b, g, c: (d, rows(d, b, c))),
            pl.BlockSpec((heads, chunk), lambda d, b, g, c: (2 + d, rows(d, b, c))),
        ],
        out_specs=pl.BlockSpec((None, chunk, hps * dv), lambda d, b, g, c: (d, rows(d, b, c), g)),
        out_shape=jax.ShapeDtypeStruct((2, t, heads * dv), F32),
        scratch_shapes=[pltpu.VMEM((hps, dqk, dv + LANES), F32), pltpu.VMEM((hps, 8, LANES), F32)],
        compiler_params=_params("parallel", "parallel", "parallel", "arbitrary"),
        name="mlstm",
    )(b_ig, b_fg, p, p, p, gates_t, gates_t)


def _head_norm_kernel(h_ref, og_ref, g_ref, o_ref):
    hs = h_ref[0] + h_ref[1]
    ms = jnp.mean(hs * hs, axis=-1, keepdims=True)
    y = hs * lax.rsqrt(ms + EPS) * g_ref[...]
    o_ref[...] = (y * jax.nn.sigmoid(og_ref[...].astype(F32))).astype(BF16)


def _head_norm(h2, p, gain, *, heads, dv, ooff):
    _, t, vw = h2.shape
    tm = _tile(t, 1024)
    return pl.pallas_call(
        _head_norm_kernel,
        grid=(t // tm, heads),
        in_specs=[
            pl.BlockSpec((2, tm, dv), lambda i, h: (0, i, h)),
            pl.BlockSpec((tm, dv), lambda i, h: (i, ooff // dv + h)),
            pl.BlockSpec((1, dv), lambda i, h: (0, h)),
        ],
        out_specs=pl.BlockSpec((tm, dv), lambda i, h: (i, h)),
        out_shape=jax.ShapeDtypeStruct((t, vw), BF16),
        compiler_params=_params("parallel", "parallel"),
        name="head_norm",
    )(h2, p, gain.reshape(1, vw))


def _twiddles(n, cols):
    r = math.gcd(n, 64)
    k = jnp.arange(cols, dtype=jnp.int32)[None, :]

    def cos_sin(j):
        ang = ((j[:, None] * k) % n).astype(F32) * (2.0 * math.pi / n)
        return jnp.cos(ang), jnp.sin(ang)

    ac, asn = cos_sin(jnp.arange(n // r, dtype=jnp.int32) * r)
    bc, bsn = cos_sin(jnp.arange(r, dtype=jnp.int32))
    cos = (ac[:, None, :] * bc[None] - asn[:, None, :] * bsn[None]).reshape(n, cols)
    sin = (asn[:, None, :] * bc[None] + ac[:, None, :] * bsn[None]).reshape(n, cols)
    return cos, sin


def _dft_fold_kernel(xd_ref, xa_ref, xb_ref, w_ref, o_ref, *, tm, gdim):
    u = lax.broadcasted_iota(jnp.int32, (tm, 2 * tm), 0)
    v = lax.broadcasted_iota(jnp.int32, (tm, 2 * tm), 1)
    pick = (v == tm - u).astype(BF16)
    pair = jnp.concatenate([xa_ref[...], xb_ref[...]], axis=0)
    xr = jnp.dot(pick, pair, preferred_element_type=F32)
    xd = xd_ref[...].astype(F32)
    w = w_ref[...]
    o_ref[:, :gdim] = jnp.dot((xd + xr).astype(BF16), w[:, :gdim], preferred_element_type=F32).astype(BF16)
    o_ref[:, gdim:] = jnp.dot((xd - xr).astype(BF16), w[:, gdim:], preferred_element_type=F32).astype(BF16)


def _dft_fold(p, w_cs, *, batch, seq, groups, gdim, froff):
    half = seq // 2
    tm = _tile(half, 256)
    nt, nh = seq // tm, half // tm
    cb = froff // gdim
    return pl.pallas_call(
        functools.partial(_dft_fold_kernel, tm=tm, gdim=gdim),
        grid=(batch, nh, groups),
        in_specs=[
            pl.BlockSpec((tm, gdim), lambda b, i, g: (b * nt + i, cb + g)),
            pl.BlockSpec((tm, gdim), lambda b, i, g: (b * nt + nt - 1 - i, cb + g)),
            pl.BlockSpec((tm, gdim), lambda b, i, g: (b * nt + (nt - i) % nt, cb + g)),
            pl.BlockSpec((gdim, 2 * gdim), lambda b, i, g: (0, 0)),
        ],
        out_specs=pl.BlockSpec((tm, 2 * gdim), lambda b, i, g: (b * nh + i, g)),
        out_shape=jax.ShapeDtypeStruct((batch * half, 2 * groups * gdim), BF16),
        compiler_params=_params("parallel", "parallel", "parallel"),
        name="dft_fold",
    )(p, p, p, w_cs)


def _dft_pos_kernel(c_ref, s_ref, e_ref, om_ref, xh_ref, wc_ref, o_ref, yh_ref, *, scale, tk):
    i = pl.program_id(2)

    @pl.when(i == 0)
    def _():
        yh_ref[...] = jnp.dot(xh_ref[...], wc_ref[...], preferred_element_type=F32)

    acc = jnp.dot(c_ref[...], e_ref[...], preferred_element_type=F32)
    acc = acc + jnp.dot(s_ref[...], om_ref[...], preferred_element_type=F32)
    k1 = i * tk + lax.broadcasted_iota(jnp.int32, (tk, 1), 0)
    sgn = (1 - 2 * (k1 & 1)).astype(F32)
    o_ref[...] = ((acc + sgn * yh_ref[0:1, :]) * scale).astype(BF16)


def _dft_pos(y, p, cos_w, nsin, w_cs, *, batch, seq, groups, gdim, froff):
    t = batch * seq
    half = seq // 2
    tk = _tile(seq, 1024)
    nk = seq // tk
    cb = froff // gdim
    scale = 1.0 / math.sqrt(seq * gdim)
    return pl.pallas_call(
        functools.partial(_dft_pos_kernel, scale=scale, tk=tk),
        grid=(batch, groups, nk),
        in_specs=[
            pl.BlockSpec((tk, half), lambda b, g, i: (i, 0)),
            pl.BlockSpec((tk, half), lambda b, g, i: (i, 0)),
            pl.BlockSpec((half, gdim), lambda b, g, i: (b, 2 * g), pipeline_mode=pl.Buffered(1)),
            pl.BlockSpec((half, gdim), lambda b, g, i: (b, 2 * g + 1), pipeline_mode=pl.Buffered(1)),
            pl.BlockSpec((BF16_SUBLANES, gdim),
                         lambda b, g, i: ((b * seq + half) // BF16_SUBLANES, cb + g)),
            pl.BlockSpec((gdim, gdim), lambda b, g, i: (0, 0)),
        ],
        out_specs=pl.BlockSpec((tk, gdim), lambda b, g, i: (b * nk + i, g)),
        out_shape=jax.ShapeDtypeStruct((t, groups * gdim), BF16),
        scratch_shapes=[pltpu.VMEM((BF16_SUBLANES, gdim), F32)],
        compiler_params=_params("parallel", "parallel", "arbitrary"),
        name="dft_pos",
    )(cos_w, nsin, y, y, p, w_cs)


def _merge_kernel(hn_ref, fr_ref, wmo_ref, wfo_ref, gm_ref, gf_ref, bm_ref, bf_ref, o_ref):
    hm = jnp.dot(hn_ref[...], wmo_ref[...], preferred_element_type=F32)
    hf = jnp.dot(fr_ref[...], wfo_ref[...], preferred_element_type=F32)
    g1 = jax.nn.sigmoid(gm_ref[...].astype(F32) + bm_ref[...])
    g2 = jax.nn.sigmoid(gf_ref[...].astype(F32) + bf_ref[...])
    o_ref[...] = (g1 * hm + g2 * hf).astype(BF16)


def _merge(hn, fr, w_mo, w_fo, p, b_merge, *, gmoff):
    t, vw = hn.shape
    fw = fr.shape[1]
    d = w_mo.shape[1]
    tm, tn = _tile(t, 1024), _tile(d, 512)
    nj = d // tn
    bm = b_merge.reshape(1, 2 * d)
    return pl.pallas_call(
        _merge_kernel,
        grid=(t // tm, nj),
        in_specs=[
            pl.BlockSpec((tm, vw), lambda i, j: (i, 0), pipeline_mode=pl.Buffered(1)),
            pl.BlockSpec((tm, fw), lambda i, j: (i, 0), pipeline_mode=pl.Buffered(1)),
            pl.BlockSpec((vw, tn), lambda i, j: (0, j)),
            pl.BlockSpec((fw, tn), lambda i, j: (0, j)),
            pl.BlockSpec((tm, tn), lambda i, j: (i, gmoff // tn + j)),
            pl.BlockSpec((tm, tn), lambda i, j: (i, gmoff // tn + nj + j)),
            pl.BlockSpec((1, tn), lambda i, j: (0, j)),
            pl.BlockSpec((1, tn), lambda i, j: (0, nj + j)),
        ],
        out_specs=pl.BlockSpec((tm, tn), lambda i, j: (i, j)),
        out_shape=jax.ShapeDtypeStruct((t, d), BF16),
        compiler_params=_params("parallel", "arbitrary"),
        name="merge",
    )(hn, fr, w_mo, w_fo, p, p, bm, bm)


def _proj_res_kernel(a_ref, w_ref, r_ref, o_ref):
    o_ref[...] = r_ref[...] + jnp.dot(a_ref[...], w_ref[...], preferred_element_type=F32)


def _proj_res(a, w, res, *, tn, name):
    t, kdim = a.shape
    n = w.shape[1]
    tm, tn = _tile(t, 1024), _tile(n, tn)
    return pl.pallas_call(
        _proj_res_kernel,
        grid=(t // tm, n // tn),
        in_specs=[
            pl.BlockSpec((tm, kdim), lambda i, j: (i, 0), pipeline_mode=pl.Buffered(1)),
            pl.BlockSpec((kdim, tn), lambda i, j: (0, j)),
            pl.BlockSpec((tm, tn), lambda i, j: (i, j)),
        ],
        out_specs=pl.BlockSpec((tm, tn), lambda i, j: (i, j)),
        out_shape=jax.ShapeDtypeStruct((t, n), F32),
        compiler_params=_params("parallel", "arbitrary"),
        name=name,
    )(a, w, res)


def _conv_glu_kernel(a_ref, ap_ref, an_ref, v_ref, vp_ref, vn_ref, cwa_ref, cwv_ref, cba_ref, cbv_ref,
                     o_ref, *, tm, seq):
    r0 = pl.program_id(0) * tm
    at_start = (r0 % seq) == 0
    at_end = ((r0 + tm) % seq) == 0
    rows = lax.broadcasted_iota(jnp.int32, a_ref.shape, 0)

    def conv(x_ref, p_ref, n_ref, cw_ref, cb_ref):
        x = x_ref[...].astype(F32)
        prev = jnp.where(at_start, 0.0, p_ref[BF16_SUBLANES - 1:BF16_SUBLANES, :].astype(F32))
        nxt = jnp.where(at_end, 0.0, n_ref[0:1, :].astype(F32))
        up = jnp.where(rows == 0, prev, pltpu.roll(x, 1, 0))
        dn = jnp.where(rows == tm - 1, nxt, pltpu.roll(x, tm - 1, 0))
        cw = cw_ref[...]
        return cw[0:1] * up + cw[1:2] * x + cw[2:3] * dn + cb_ref[...]

    a = conv(a_ref, ap_ref, an_ref, cwa_ref, cba_ref)
    val = conv(v_ref, vp_ref, vn_ref, cwv_ref, cbv_ref)
    gelu = 0.5 * a * (1.0 + lax.erf(a * math.sqrt(0.5)))
    o_ref[...] = (gelu * val).astype(BF16)


def _conv_glu(u, cw, cb, *, seq):
    t, n2 = u.shape
    ff = n2 // 2
    tm, tc = _tile(seq, 512), _tile(ff, 1024)
    nj = ff // tc
    hb = tm // BF16_SUBLANES
    last_hb = t // BF16_SUBLANES - 1
    cb2 = cb.reshape(1, n2)

    def prev_map(off):
        return lambda i, j: (jnp.maximum(i * hb - 1, 0), off + j)

    def next_map(off):
        return lambda i, j: (jnp.minimum((i + 1) * hb, last_hb), off + j)

    def main_map(off):
        return lambda i, j: (i, off + j)

    in_specs = []
    for off in (0, nj):
        in_specs += [pl.BlockSpec((tm, tc), main_map(off)),
                     pl.BlockSpec((BF16_SUBLANES, tc), prev_map(off)),
                     pl.BlockSpec((BF16_SUBLANES, tc), next_map(off))]
    in_specs += [pl.BlockSpec((CONV_W, tc), lambda i, j: (0, j)),
                 pl.BlockSpec((CONV_W, tc), lambda i, j: (0, nj + j)),
                 pl.BlockSpec((1, tc), lambda i, j: (0, j)),
                 pl.BlockSpec((1, tc), lambda i, j: (0, nj + j))]
    return pl.pallas_call(
        functools.partial(_conv_glu_kernel, tm=tm, seq=seq),
        grid=(t // tm, nj),
        in_specs=in_specs,
        out_specs=pl.BlockSpec((tm, tc), lambda i, j: (i, j)),
        out_shape=jax.ShapeDtypeStruct((t, ff), BF16),
        compiler_params=_params("parallel", "parallel"),
        name="conv_glu",
    )(u, u, u, u, u, u, cw, cw, cb2, cb2)


def _final_norm_kernel(x_ref, g_ref, o_ref):
    x = x_ref[...]
    ms = jnp.mean(x * x, axis=-1, keepdims=True)
    o_ref[...] = x * lax.rsqrt(ms + EPS) * g_ref[...]


def _final_norm(x, gain):
    t, d = x.shape
    tm = _tile(t, 256)
    return pl.pallas_call(
        _final_norm_kernel,
        grid=(t // tm,),
        in_specs=[pl.BlockSpec((tm, d), lambda i: (i, 0)), pl.BlockSpec((1, d), lambda i: (0, 0))],
        out_specs=pl.BlockSpec((tm, d), lambda i: (i, 0)),
        out_shape=jax.ShapeDtypeStruct((t, d), F32),
        compiler_params=_params("parallel"),
        name="final_norm",
    )(x, gain.reshape(1, d))


def _cast_skip_kernel(a_ref, b_ref, o_ref, *, first_shifted, shift):
    a = a_ref[...]
    shifted = jnp.concatenate([a[:, shift:], b_ref[:, :shift]], axis=1)
    o_ref[...] = jnp.where(pl.program_id(1) >= first_shifted, shifted, a).astype(BF16)


def _cast_skip_cols(w, skip_at, skip):
    rows, cols = w.shape
    n_out = cols - skip
    tc = next(c for c in (1024, 512, 256, LANES) if skip_at % c == 0 and n_out % c == 0)
    tr = _tile(rows, 1024)
    per = tc // LANES
    return pl.pallas_call(
        functools.partial(_cast_skip_kernel, first_shifted=skip_at // tc, shift=skip),
        grid=(rows // tr, n_out // tc),
        in_specs=[pl.BlockSpec((tr, tc), lambda i, j: (i, j)),
                  pl.BlockSpec((tr, LANES), lambda i, j: (i, (j + 1) * per))],
        out_specs=pl.BlockSpec((tr, tc), lambda i, j: (i, j)),
        out_shape=jax.ShapeDtypeStruct((rows, n_out), BF16),
        compiler_params=_params("parallel", "parallel"),
        name="cast_skip_cols",
    )(w, w)


def _cast_pad_kernel(a_ref, o_ref, *, axis, real_tiles, period):
    is_pad = (pl.program_id(axis) % period) >= real_tiles
    o_ref[...] = jnp.where(is_pad, 0.0, a_ref[...]).astype(BF16)


def _cast_pad_halves(w, ff, ffp, *, axis):
    tile = 256
    assert ff % tile == 0 and ffp % tile == 0 and w.shape[axis] % ff == 0
    real, period = ff // tile, ffp // tile
    sections = w.shape[axis] // ff
    other = w.shape[1 - axis]
    to = _tile(other, 4096)

    def src(t):
        return (t // period) * real + jnp.minimum(t % period, real - 1)

    if axis == 1:
        block, grid = (to, tile), (other // to, sections * period)
        in_map = lambda i, j: (i, src(j))
        out_shape = (other, sections * ffp)
    else:
        block, grid = (tile, to), (sections * period, other // to)
        in_map = lambda i, j: (src(i), j)
        out_shape = (sections * ffp, other)
    return pl.pallas_call(
        functools.partial(_cast_pad_kernel, axis=1 if axis == 1 else 0, real_tiles=real, period=period),
        grid=grid,
        in_specs=[pl.BlockSpec(block, in_map)],
        out_specs=pl.BlockSpec(block, lambda i, j: (i, j)),
        out_shape=jax.ShapeDtypeStruct(out_shape, BF16),
        compiler_params=_params("parallel", "parallel"),
        name="cast_pad",
    )(w)


def _pad_cols(a, n):
    return jnp.pad(a, ((0, 0), (0, n - a.shape[1])))


def _prep_weights(w_in, b_igate, b_fgate, w_mlstm_out, w_fourier_out, w_out, w_up, conv_w, conv_b, w_down,
                  *, heads, vw, fw):
    d, in_w = w_in.shape[1:]
    qk_w = (in_w - 2 * vw - 4 * heads - fw - 2 * d) // 2
    off_i = 2 * qk_w + 2 * vw
    off_fr = off_i + 4 * heads
    w = w_in[0]
    w_main = _cast_skip_cols(w, off_i, off_fr - off_i)
    w_gate_t = jnp.pad(w[:, off_i:off_fr].T, ((0, GATE_ROWS - 4 * heads), (0, 0))).astype(BF16)
    ff = w_down.shape[1]
    ffp = -(-ff // 1024) * 1024
    w_up_p = _cast_pad_halves(w_up[0], ff, ffp, axis=1)
    cw_p = jnp.concatenate([_pad_cols(conv_w[0][:, :ff], ffp), _pad_cols(conv_w[0][:, ff:], ffp)], axis=1)
    cb_p = jnp.concatenate([jnp.pad(conv_b[0][:ff], (0, ffp - ff)), jnp.pad(conv_b[0][ff:], (0, ffp - ff))])
    w_down_p = _cast_pad_halves(w_down[0], ff, ffp, axis=0)
    return dict(
        w_main=w_main, w_gate_t=w_gate_t, qk_w=qk_w,
        b_ig=b_igate[0].reshape(-1), b_fg=b_fgate[0].reshape(-1),
        w_mo=w_mlstm_out[0].astype(BF16), w_fo=w_fourier_out[0].astype(BF16), w_out=w_out[0].astype(BF16),
        w_up=w_up_p, cw=cw_p, cb=cb_p, w_down=w_down_p, ffp=ffp,
    )


def _trunk(x, wts, norm_mix, mh_norm, b_merge, norm_ffn, norm_final, *, heads, groups):
    batch, seq, d = x.shape
    t = batch * seq
    x2 = x.reshape(t, d)
    vw = wts["w_mo"].shape[0]
    fw = wts["w_fo"].shape[0]
    qk_w = wts["qk_w"]
    dqk, dv, gdim = qk_w // heads, vw // heads, fw // groups
    ooff = 2 * qk_w + vw
    froff = ooff + vw
    gmoff = froff + fw

    p, gates_t = _norm_proj(x2, norm_mix[0], wts["w_main"], wts["w_gate_t"], name="in_proj")
    h2 = _mlstm(p, gates_t, wts["b_ig"], wts["b_fg"], batch=batch, seq=seq, heads=heads, dqk=dqk, dv=dv)
    hn = _head_norm(h2, p, mh_norm[0], heads=heads, dv=dv, ooff=ooff)

    cos_c, sin_c = _twiddles(gdim, gdim)
    w_cs = jnp.concatenate([cos_c, sin_c], axis=1).astype(BF16)
    cos_s, sin_s = _twiddles(seq, seq // 2)
    col = lax.broadcasted_iota(jnp.int32, cos_s.shape, 1)
    cos_w = jnp.where(col == 0, 0.5, cos_s).astype(BF16)
    y = _dft_fold(p, w_cs, batch=batch, seq=seq, groups=groups, gdim=gdim, froff=froff)
    fr = _dft_pos(y, p, cos_w, (-sin_s).astype(BF16), w_cs,
                  batch=batch, seq=seq, groups=groups, gdim=gdim, froff=froff)

    merged = _merge(hn, fr, wts["w_mo"], wts["w_fo"], p, b_merge[0], gmoff=gmoff)
    x1 = _proj_res(merged, wts["w_out"], x2, tn=512, name="out_proj")

    u = _norm_proj(x1, norm_ffn[0], wts["w_up"], name="up_proj")
    hf = _conv_glu(u, wts["cw"], wts["cb"], seq=seq)
    ffp = wts["ffp"]
    x3 = _proj_res(hf, wts["w_down"], x1, tn=256, name="down_proj")
    return _final_norm(x3, norm_final).reshape(batch, seq, d)


def kernel(x_prompt, x_sample, norm_mix, w_in, b_igate, b_fgate, mh_norm, w_mlstm_out, w_fourier_out, b_merge, w_out, norm_ffn, w_up, conv_w, conv_b, w_down, norm_final):
    heads = b_igate.shape[-1]
    vw, fw = w_mlstm_out.shape[1], w_fourier_out.shape[1]
    wts = _prep_weights(w_in, b_igate, b_fgate, w_mlstm_out, w_fourier_out, w_out, w_up, conv_w, conv_b, w_down,
                        heads=heads, vw=vw, fw=fw)
    run = functools.partial(_trunk, wts=wts, norm_mix=norm_mix, mh_norm=mh_norm, b_merge=b_merge,
                            norm_ffn=norm_ffn, norm_final=norm_final, heads=heads, groups=F_GROUPS)
    return (run(x_prompt), run(x_sample))
```

```python
import functools
import math

import jax
import jax.numpy as jnp
from jax import lax
from jax.experimental import pallas as pl
from jax.experimental.pallas import tpu as pltpu

F32 = jnp.float32
BF16 = jnp.bfloat16

EPS = 1e-6
F_GROUPS = 4
CONV_W = 3
M_CHUNK = 256
M_HEADS_PER_STEP = 4

V7X_VMEM_BYTES = 64 * 1024 * 1024
VMEM_LIMIT = V7X_VMEM_BYTES - 8 * 1024 * 1024
LANES = 128
BF16_SUBLANES = 16
GATE_ROWS = 128
NORM_SLAB = 128


def _tile(n, pref):
    return pref if n % pref == 0 else n


def _params(*sem):
    return pltpu.CompilerParams(dimension_semantics=sem, vmem_limit_bytes=VMEM_LIMIT)


def _norm_rows(x_hbm, g_ref, xbuf, sem, xn_ref, row0, *, n_rows, dst0):
    slab = xbuf.shape[1]
    n_slabs = n_rows // slab

    def copy(r, slot):
        src = x_hbm.at[pl.ds(pl.multiple_of(row0 + r * slab, BF16_SUBLANES), slab), :]
        return pltpu.make_async_copy(src, xbuf.at[slot], sem.at[slot])

    copy(0, 0).start()

    def body(r, carry):
        slot = r % 2

        @pl.when(r + 1 < n_slabs)
        def _():
            copy(r + 1, 1 - slot).start()

        copy(r, slot).wait()
        x = xbuf[slot]
        ms = jnp.mean(x * x, axis=-1, keepdims=True)
        rows = pl.ds(pl.multiple_of(dst0 + r * slab, BF16_SUBLANES), slab)
        xn_ref[rows, :] = (x * lax.rsqrt(ms + EPS) * g_ref[...]).astype(BF16)
        return carry

    lax.fori_loop(0, n_slabs, body, 0)


def _in_proj_kernel(x_hbm, g_ref, w_ref, wg_ref, o_ref, gt_ref, xn_ref, xbuf, sem, *, tm):
    @pl.when(pl.program_id(1) == 0)
    def _():
        _norm_rows(x_hbm, g_ref, xbuf, sem, xn_ref, pl.program_id(0) * tm, n_rows=tm, dst0=0)
        gt_ref[...] = lax.dot_general(wg_ref[...], xn_ref[...], (((1,), (1,)), ((), ())),
                                      preferred_element_type=F32)

    o_ref[...] = jnp.dot(xn_ref[...], w_ref[...], preferred_element_type=F32).astype(o_ref.dtype)


def _in_proj(x, gain, w, w_gate):
    t, d = x.shape
    n = w.shape[1]
    tm, tn = _tile(t, 1024), _tile(n, 1024)
    slab = min(NORM_SLAB, tm)
    return pl.pallas_call(
        functools.partial(_in_proj_kernel, tm=tm),
        grid=(t // tm, n // tn),
        in_specs=[
            pl.BlockSpec(memory_space=pl.ANY),
            pl.BlockSpec((1, d), lambda i, j: (0, 0)),
            pl.BlockSpec((d, tn), lambda i, j: (0, j)),
            pl.BlockSpec((GATE_ROWS, d), lambda i, j: (0, 0)),
        ],
        out_specs=[pl.BlockSpec((tm, tn), lambda i, j: (i, j)),
                   pl.BlockSpec((GATE_ROWS, tm), lambda i, j: (0, i))],
        out_shape=[jax.ShapeDtypeStruct((t, n), BF16), jax.ShapeDtypeStruct((GATE_ROWS, t), F32)],
        scratch_shapes=[pltpu.VMEM((tm, d), BF16), pltpu.VMEM((2, slab, d), F32),
                        pltpu.SemaphoreType.DMA((2,))],
        compiler_params=_params("parallel", "arbitrary"),
        name="in_proj",
    )(x, gain.reshape(1, d), w, w_gate)


HALO = BF16_SUBLANES
EPI_ROWS = 32


def _ffn_up_kernel(x_hbm, g_ref, wa_ref, wv_ref, cwa_ref, cwv_ref, cba_ref, cbv_ref, o_ref,
                   xn_ref, xbuf, hbuf, sem, hsem, ra0, rv0, ra1, rv1, *, tm, nj, n_tiles, seq, t_rows):
    s = pl.program_id(0)
    i = s // nj
    r0 = i * tm

    @pl.when(s == 0)
    def _():
        ra1[...] = jnp.zeros_like(ra1)
        rv1[...] = jnp.zeros_like(rv1)

    @pl.when(jnp.logical_and(s % nj == 0, s < n_tiles))
    def _():
        tops = (jnp.maximum(r0 - HALO, 0), jnp.minimum(r0 + tm, t_rows - HALO))
        halo = [pltpu.make_async_copy(x_hbm.at[pl.ds(pl.multiple_of(tops[e], HALO), HALO), :],
                                      hbuf.at[e], hsem.at[e]) for e in (0, 1)]
        for cp in halo:
            cp.start()
        _norm_rows(x_hbm, g_ref, xbuf, sem, xn_ref, r0, n_rows=tm, dst0=HALO)
        for e, cp in enumerate(halo):
            cp.wait()
            x = hbuf[e]
            ms = jnp.mean(x * x, axis=-1, keepdims=True)
            dst = 0 if e == 0 else HALO + tm
            xn_ref[dst:dst + HALO, :] = (x * lax.rsqrt(ms + EPS) * g_ref[...]).astype(BF16)

    at_start = (r0 % seq) == 0
    at_end = ((r0 + tm) % seq) == 0

    def step(wr_a, wr_v, rd_a, rd_v):
        def conv(rd, cw_ref, cb_ref, r):
            cw = cw_ref[...]
            lo = HALO + r
            return (cw[0:1] * rd[lo - 1:lo - 1 + EPI_ROWS, :] + cw[1:2] * rd[lo:lo + EPI_ROWS, :]
                    + cw[2:3] * rd[lo + 1:lo + 1 + EPI_ROWS, :] + cb_ref[...])

        for r in range(0, tm, EPI_ROWS):
            a = conv(rd_a, cwa_ref, cba_ref, r)
            val = conv(rd_v, cwv_ref, cbv_ref, r)
            o_ref[r:r + EPI_ROWS, :] = (0.5 * a * (1.0 + lax.erf(a * math.sqrt(0.5))) * val).astype(BF16)

        xn = xn_ref[...]
        wr_a[...] = jnp.dot(xn, wa_ref[...], preferred_element_type=F32)
        wr_v[...] = jnp.dot(xn, wv_ref[...], preferred_element_type=F32)

        @pl.when(at_start)
        def _():
            wr_a[HALO - 1:HALO, :] = jnp.zeros((1, wr_a.shape[1]), F32)
            wr_v[HALO - 1:HALO, :] = jnp.zeros((1, wr_v.shape[1]), F32)

        @pl.when(at_end)
        def _():
            wr_a[HALO + tm:HALO + tm + 1, :] = jnp.zeros((1, wr_a.shape[1]), F32)
            wr_v[HALO + tm:HALO + tm + 1, :] = jnp.zeros((1, wr_v.shape[1]), F32)

    @pl.when(s % 2 == 0)
    def _():
        step(ra0, rv0, ra1, rv1)

    @pl.when(s % 2 == 1)
    def _():
        step(ra1, rv1, ra0, rv0)


def _ffn_up(x, gain, w, cw, cb, *, seq):
    t, d = x.shape
    ff = w.shape[1] // 2
    tm, tn = _tile(seq, 1024), _tile(ff, 512)
    nj = ff // tn
    n_tiles = (t // tm) * nj
    slab = min(NORM_SLAB, tm)
    cb2 = cb.reshape(1, 2 * ff)

    def mm_tile(s):
        return jnp.minimum(s, n_tiles - 1) % nj

    def ep_tile(s):
        return jnp.maximum(s - 1, 0)

    raw = pltpu.VMEM((tm + 2 * HALO, tn), F32)
    return pl.pallas_call(
        functools.partial(_ffn_up_kernel, tm=tm, nj=nj, n_tiles=n_tiles, seq=seq, t_rows=t),
        grid=(n_tiles + 1,),
        in_specs=[
            pl.BlockSpec(memory_space=pl.ANY),
            pl.BlockSpec((1, d), lambda s: (0, 0)),
            pl.BlockSpec((d, tn), lambda s: (0, mm_tile(s))),
            pl.BlockSpec((d, tn), lambda s: (0, nj + mm_tile(s))),
            pl.BlockSpec((CONV_W, tn), lambda s: (0, ep_tile(s) % nj)),
            pl.BlockSpec((CONV_W, tn), lambda s: (0, nj + ep_tile(s) % nj)),
            pl.BlockSpec((1, tn), lambda s: (0, ep_tile(s) % nj)),
            pl.BlockSpec((1, tn), lambda s: (0, nj + ep_tile(s) % nj)),
        ],
        out_specs=pl.BlockSpec((tm, tn), lambda s: (ep_tile(s) // nj, ep_tile(s) % nj)),
        out_shape=jax.ShapeDtypeStruct((t, ff), BF16),
        scratch_shapes=[pltpu.VMEM((tm + 2 * HALO, d), BF16), pltpu.VMEM((2, slab, d), F32),
                        pltpu.VMEM((2, HALO, d), F32), pltpu.SemaphoreType.DMA((2,)),
                        pltpu.SemaphoreType.DMA((2,)), raw, raw, raw, raw],
        compiler_params=_params("arbitrary"),
        name="ffn_up",
    )(x, gain.reshape(1, d), w, w, cw, cw, cb2, cb2)


def _mlstm_kernel(big_ref, bfg_ref, q_ref, k_ref, v_ref, gi_ref, gf_ref, o_ref, ct_ref, m_ref,
                  *, heads, hps, dqk, dv, chunk):
    d = pl.program_id(0)
    hg = pl.program_id(2)
    c = pl.program_id(3)

    @pl.when(c == 0)
    def _():
        ct_ref[...] = jnp.zeros_like(ct_ref)
        m_ref[...] = jnp.zeros_like(m_ref)

    sign = 1 - 2 * d
    row = lax.broadcasted_iota(jnp.int32, (chunk, chunk), 0)
    col = lax.broadcasted_iota(jnp.int32, (chunk, chunk), 1)
    before = ((row - col) * sign <= 0).astype(BF16)
    mask = ((col - row) * sign <= 0)
    diag = row == col
    prow = lax.broadcasted_iota(jnp.int32, (BF16_SUBLANES, chunk), 0)
    ones_col = (lax.broadcasted_iota(jnp.int32, (chunk, LANES), 1) == 0).astype(BF16)
    kscale = dqk ** -0.5

    for u in range(hps):
        h = hg * hps + u
        gidx = d * heads + h
        i_row = gi_ref[pl.ds(h, 1), :] + big_ref[gidx]
        xf = gf_ref[pl.ds(h, 1), :] + bfg_ref[gidx]
        lf_row = jnp.minimum(xf, 0.0) - jnp.log1p(jnp.exp(-jnp.abs(xf)))

        hi = lf_row.astype(BF16).astype(F32)
        r1 = lf_row - hi
        mid = r1.astype(BF16).astype(F32)
        lo = r1 - mid
        parts = jnp.where(prow == 0, hi, jnp.where(prow == 1, mid, jnp.where(prow == 2, lo, 0.0)))
        b3 = jnp.dot(parts.astype(BF16), before, preferred_element_type=F32)
        b_row = b3[0:1] + b3[1:2] + b3[2:3]
        g_tot = jnp.sum(lf_row, axis=1, keepdims=True)

        r_row = i_row - b_row
        m_prev = m_ref[u, 0:1, 0:1]
        rm = jnp.where(mask, r_row, -jnp.inf)
        m_col = jnp.maximum(jnp.max(rm, axis=1, keepdims=True), m_prev)
        m_all = jnp.maximum(jnp.max(r_row, axis=1, keepdims=True), m_prev)
        dmat = jnp.exp(rm - m_col)

        q = q_ref[:, u * dqk:(u + 1) * dqk]
        k = k_ref[:, u * dqk:(u + 1) * dqk]
        v_aug = jnp.concatenate([v_ref[:, u * dv:(u + 1) * dv], ones_col], axis=1)

        qk = lax.dot_general(q, k, (((1,), (1,)), ((), ())), preferred_element_type=F32)
        s = (qk * kscale * dmat).astype(BF16)
        ct = ct_ref[u]
        inter = jnp.dot(q, ct.astype(BF16), preferred_element_type=F32)
        sc = jnp.exp(m_prev - m_col) * kscale
        numden = sc * inter + jnp.dot(s, v_aug, preferred_element_type=F32)
        num = numden[:, :dv]
        den = numden[:, dv:dv + 1]
        b_col = jnp.sum(jnp.where(diag, b_row, 0.0), axis=1, keepdims=True)
        floor = jnp.exp(-(b_col + m_col))
        o_ref[:, u * dv:(u + 1) * dv] = num * (1.0 / jnp.maximum(jnp.abs(den), floor))

        wa_row = jnp.exp(r_row - m_all)
        decay = jnp.exp(m_prev - m_all)
        ktw = (k.astype(F32).T * wa_row).astype(BF16)
        ct_ref[u] = decay * ct + jnp.dot(ktw, v_aug, preferred_element_type=F32)
        m_ref[u] = jnp.broadcast_to(g_tot + m_all, m_ref.shape[1:])


def _mlstm(p, gates_t, b_ig, b_fg, *, batch, seq, heads, dqk, dv):
    t = batch * seq
    chunk = _tile(seq, M_CHUNK)
    nc = seq // chunk
    hps = M_HEADS_PER_STEP
    ng = heads // hps
    koff = ng
    voff = 2 * heads * dqk // (hps * dv)

    def rows(d, b, c):
        return b * nc + c + d * (nc - 1 - 2 * c)

    smem = pl.BlockSpec(memory_space=pltpu.SMEM)
    return pl.pallas_call(
        functools.partial(_mlstm_kernel, heads=heads, hps=hps, dqk=dqk, dv=dv, chunk=chunk),
        grid=(2, batch, ng, nc),
        in_specs=[
            smem, smem,
            pl.BlockSpec((chunk, hps * dqk), lambda d, b, g, c: (rows(d, b, c), g)),
            pl.BlockSpec((chunk, hps * dqk), lambda d, b, g, c: (rows(d, b, c), koff + g)),
            pl.BlockSpec((chunk, hps * dv), lambda d, b, g, c: (rows(d, b, c), voff + g)),
            pl.BlockSpec((heads, chunk), lambda d, b, g, c: (d, rows(d, b, c))),
            pl.BlockSpec((heads, chunk), lambda d, b, g, c: (2 + d, rows(d, b, c))),
        ],
        out_specs=pl.BlockSpec((None, chunk, hps * dv), lambda d, b, g, c: (d, rows(d, b, c), g)),
        out_shape=jax.ShapeDtypeStruct((2, t, heads * dv), F32),
        scratch_shapes=[pltpu.VMEM((hps, dqk, dv + LANES), F32), pltpu.VMEM((hps, 8, LANES), F32)],
        compiler_params=_params("parallel", "parallel", "parallel", "arbitrary"),
        name="mlstm",
    )(b_ig, b_fg, p, p, p, gates_t, gates_t)


def _head_norm_kernel(h_ref, og_ref, g_ref, o_ref):
    hs = h_ref[0] + h_ref[1]
    ms = jnp.mean(hs * hs, axis=-1, keepdims=True)
    y = hs * lax.rsqrt(ms + EPS) * g_ref[...]
    o_ref[...] = (y * jax.nn.sigmoid(og_ref[...].astype(F32))).astype(BF16)


def _head_norm(h2, p, gain, *, heads, dv, ooff):
    _, t, vw = h2.shape
    tm = _tile(t, 1024)
    return pl.pallas_call(
        _head_norm_kernel,
        grid=(t // tm, heads),
        in_specs=[
            pl.BlockSpec((2, tm, dv), lambda i, h: (0, i, h)),
            pl.BlockSpec((tm, dv), lambda i, h: (i, ooff // dv + h)),
            pl.BlockSpec((1, dv), lambda i, h: (0, h)),
        ],
        out_specs=pl.BlockSpec((tm, dv), lambda i, h: (i, h)),
        out_shape=jax.ShapeDtypeStruct((t, vw), BF16),
        compiler_params=_params("parallel", "parallel"),
        name="head_norm",
    )(h2, p, gain.reshape(1, vw))


def _twiddles(n, cols):
    r = math.gcd(n, 64)
    k = jnp.arange(cols, dtype=jnp.int32)[None, :]

    def cos_sin(j):
        ang = ((j[:, None] * k) % n).astype(F32) * (2.0 * math.pi / n)
        return jnp.cos(ang), jnp.sin(ang)

    ac, asn = cos_sin(jnp.arange(n // r, dtype=jnp.int32) * r)
    bc, bsn = cos_sin(jnp.arange(r, dtype=jnp.int32))
    cos = (ac[:, None, :] * bc[None] - asn[:, None, :] * bsn[None]).reshape(n, cols)
    sin = (asn[:, None, :] * bc[None] + ac[:, None, :] * bsn[None]).reshape(n, cols)
    return cos, sin


def _dft_fold_kernel(xd_ref, xa_ref, xb_ref, w_ref, o_ref, *, tm, gdim):
    u = lax.broadcasted_iota(jnp.int32, (tm, 2 * tm), 0)
    v = lax.broadcasted_iota(jnp.int32, (tm, 2 * tm), 1)
    pick = (v == tm - u).astype(BF16)
    pair = jnp.concatenate([xa_ref[...], xb_ref[...]], axis=0)
    xr = jnp.dot(pick, pair, preferred_element_type=F32)
    xd = xd_ref[...].astype(F32)
    w = w_ref[...]
    o_ref[:, :gdim] = jnp.dot((xd + xr).astype(BF16), w[:, :gdim], preferred_element_type=F32).astype(BF16)
    o_ref[:, gdim:] = jnp.dot((xd - xr).astype(BF16), w[:, gdim:], preferred_element_type=F32).astype(BF16)


def _dft_fold(p, w_cs, *, batch, seq, groups, gdim, froff):
    half = seq // 2
    tm = _tile(half, 256)
    nt, nh = seq // tm, half // tm
    cb = froff // gdim
    return pl.pallas_call(
        functools.partial(_dft_fold_kernel, tm=tm, gdim=gdim),
        grid=(batch, nh, groups),
        in_specs=[
            pl.BlockSpec((tm, gdim), lambda b, i, g: (b * nt + i, cb + g)),
            pl.BlockSpec((tm, gdim), lambda b, i, g: (b * nt + nt - 1 - i, cb + g)),
            pl.BlockSpec((tm, gdim), lambda b, i, g: (b * nt + (nt - i) % nt, cb + g)),
            pl.BlockSpec((gdim, 2 * gdim), lambda b, i, g: (0, 0)),
        ],
        out_specs=pl.BlockSpec((tm, 2 * gdim), lambda b, i, g: (b * nh + i, g)),
        out_shape=jax.ShapeDtypeStruct((batch * half, 2 * groups * gdim), BF16),
        compiler_params=_params("parallel", "parallel", "parallel"),
        name="dft_fold",
    )(p, p, p, w_cs)


def _dft_pos_kernel(c_ref, s_ref, e_ref, om_ref, xh_ref, wc_ref, o_ref, yh_ref, *, scale, tk):
    i = pl.program_id(2)

    @pl.when(i == 0)
    def _():
        yh_ref[...] = jnp.dot(xh_ref[...], wc_ref[...], preferred_element_type=F32)

    acc = jnp.dot(c_ref[...], e_ref[...], preferred_element_type=F32)
    acc = acc + jnp.dot(s_ref[...], om_ref[...], preferred_element_type=F32)
    k1 = i * tk + lax.broadcasted_iota(jnp.int32, (tk, 1), 0)
    sgn = (1 - 2 * (k1 & 1)).astype(F32)
    o_ref[...] = ((acc + sgn * yh_ref[0:1, :]) * scale).astype(BF16)


def _dft_pos(y, p, cos_w, nsin, w_cs, *, batch, seq, groups, gdim, froff):
    t = batch * seq
    half = seq // 2
    tk = _tile(seq, 1024)
    nk = seq // tk
    cb = froff // gdim
    scale = 1.0 / math.sqrt(seq * gdim)
    return pl.pallas_call(
        functools.partial(_dft_pos_kernel, scale=scale, tk=tk),
        grid=(batch, groups, nk),
        in_specs=[
            pl.BlockSpec((tk, half), lambda b, g, i: (i, 0)),
            pl.BlockSpec((tk, half), lambda b, g, i: (i, 0)),
            pl.BlockSpec((half, gdim), lambda b, g, i: (b, 2 * g), pipeline_mode=pl.Buffered(1)),
            pl.BlockSpec((half, gdim), lambda b, g, i: (b, 2 * g + 1), pipeline_mode=pl.Buffered(1)),
            pl.BlockSpec((BF16_SUBLANES, gdim),
                         lambda b, g, i: ((b * seq + half) // BF16_SUBLANES, cb + g)),
            pl.BlockSpec((gdim, gdim), lambda b, g, i: (0, 0)),
        ],
        out_specs=pl.BlockSpec((tk, gdim), lambda b, g, i: (b * nk + i, g)),
        out_shape=jax.ShapeDtypeStruct((t, groups * gdim), BF16),
        scratch_shapes=[pltpu.VMEM((BF16_SUBLANES, gdim), F32)],
        compiler_params=_params("parallel", "parallel", "arbitrary"),
        name="dft_pos",
    )(cos_w, nsin, y, y, p, w_cs)


def _merge_kernel(hn_ref, fr_ref, wmo_ref, wfo_ref, gm_ref, gf_ref, bm_ref, bf_ref, o_ref):
    hm = jnp.dot(hn_ref[...], wmo_ref[...], preferred_element_type=F32)
    hf = jnp.dot(fr_ref[...], wfo_ref[...], preferred_element_type=F32)
    g1 = jax.nn.sigmoid(gm_ref[...].astype(F32) + bm_ref[...])
    g2 = jax.nn.sigmoid(gf_ref[...].astype(F32) + bf_ref[...])
    o_ref[...] = (g1 * hm + g2 * hf).astype(BF16)


def _merge(hn, fr, w_mo, w_fo, p, b_merge, *, gmoff):
    t, vw = hn.shape
    fw = fr.shape[1]
    d = w_mo.shape[1]
    tm, tn = _tile(t, 1024), _tile(d, 512)
    nj = d // tn
    bm = b_merge.reshape(1, 2 * d)
    return pl.pallas_call(
        _merge_kernel,
        grid=(t // tm, nj),
        in_specs=[
            pl.BlockSpec((tm, vw), lambda i, j: (i, 0), pipeline_mode=pl.Buffered(1)),
            pl.BlockSpec((tm, fw), lambda i, j: (i, 0), pipeline_mode=pl.Buffered(1)),
            pl.BlockSpec((vw, tn), lambda i, j: (0, j)),
            pl.BlockSpec((fw, tn), lambda i, j: (0, j)),
            pl.BlockSpec((tm, tn), lambda i, j: (i, gmoff // tn + j)),
            pl.BlockSpec((tm, tn), lambda i, j: (i, gmoff // tn + nj + j)),
            pl.BlockSpec((1, tn), lambda i, j: (0, j)),
            pl.BlockSpec((1, tn), lambda i, j: (0, nj + j)),
        ],
        out_specs=pl.BlockSpec((tm, tn), lambda i, j: (i, j)),
        out_shape=jax.ShapeDtypeStruct((t, d), BF16),
        compiler_params=_params("parallel", "arbitrary"),
        name="merge",
    )(hn, fr, w_mo, w_fo, p, p, bm, bm)


def _proj_res_kernel(a_ref, w_ref, r_ref, o_ref):
    o_ref[...] = r_ref[...] + jnp.dot(a_ref[...], w_ref[...], preferred_element_type=F32)


def _proj_res(a, w, res, *, tn, a_buffers, name):
    t, kdim = a.shape
    n = w.shape[1]
    tm, tn = _tile(t, 1024), _tile(n, tn)
    return pl.pallas_call(
        _proj_res_kernel,
        grid=(t // tm, n // tn),
        in_specs=[
            pl.BlockSpec((tm, kdim), lambda i, j: (i, 0), pipeline_mode=pl.Buffered(a_buffers)),
            pl.BlockSpec((kdim, tn), lambda i, j: (0, j)),
            pl.BlockSpec((tm, tn), lambda i, j: (i, j)),
        ],
        out_specs=pl.BlockSpec((tm, tn), lambda i, j: (i, j)),
        out_shape=jax.ShapeDtypeStruct((t, n), F32),
        compiler_params=_params("parallel", "arbitrary"),
        name=name,
    )(a, w, res)


def _final_norm_kernel(x_ref, g_ref, o_ref):
    x = x_ref[...]
    ms = jnp.mean(x * x, axis=-1, keepdims=True)
    o_ref[...] = x * lax.rsqrt(ms + EPS) * g_ref[...]


def _final_norm(x, gain):
    t, d = x.shape
    tm = _tile(t, 256)
    return pl.pallas_call(
        _final_norm_kernel,
        grid=(t // tm,),
        in_specs=[pl.BlockSpec((tm, d), lambda i: (i, 0)), pl.BlockSpec((1, d), lambda i: (0, 0))],
        out_specs=pl.BlockSpec((tm, d), lambda i: (i, 0)),
        out_shape=jax.ShapeDtypeStruct((t, d), F32),
        compiler_params=_params("parallel"),
        name="final_norm",
    )(x, gain.reshape(1, d))


def _cast_skip_kernel(a_ref, b_ref, o_ref, *, first_shifted, shift):
    a = a_ref[...]
    shifted = jnp.concatenate([a[:, shift:], b_ref[:, :shift]], axis=1)
    o_ref[...] = jnp.where(pl.program_id(1) >= first_shifted, shifted, a).astype(BF16)


def _cast_skip_cols(w, skip_at, skip):
    _, rows, cols = w.shape
    n_out = cols - skip
    tc = next(c for c in (1024, 512, 256, LANES) if skip_at % c == 0 and n_out % c == 0)
    tr = _tile(rows, 1024)
    per = tc // LANES
    return pl.pallas_call(
        functools.partial(_cast_skip_kernel, first_shifted=skip_at // tc, shift=skip),
        grid=(rows // tr, n_out // tc),
        in_specs=[pl.BlockSpec((None, tr, tc), lambda i, j: (0, i, j)),
                  pl.BlockSpec((None, tr, LANES), lambda i, j: (0, i, (j + 1) * per))],
        out_specs=pl.BlockSpec((tr, tc), lambda i, j: (i, j)),
        out_shape=jax.ShapeDtypeStruct((rows, n_out), BF16),
        compiler_params=_params("parallel", "parallel"),
        name="cast_skip_cols",
    )(w, w)


def _cast_pad_kernel(a_ref, o_ref, *, axis, real_tiles, period):
    is_pad = (pl.program_id(axis) % period) >= real_tiles
    o_ref[...] = jnp.where(is_pad, 0.0, a_ref[...]).astype(BF16)


def _cast_pad_halves(w, ff, ffp, *, axis):
    tile = 256
    dims = w.shape[1:]
    assert ff % tile == 0 and ffp % tile == 0 and dims[axis] % ff == 0
    real, period = ff // tile, ffp // tile
    sections = dims[axis] // ff
    other = dims[1 - axis]
    to = _tile(other, 4096)

    def src(t):
        return (t // period) * real + jnp.minimum(t % period, real - 1)

    if axis == 1:
        block, grid = (to, tile), (other // to, sections * period)
        in_map = lambda i, j: (0, i, src(j))
        out_shape = (other, sections * ffp)
    else:
        block, grid = (tile, to), (sections * period, other // to)
        in_map = lambda i, j: (0, src(i), j)
        out_shape = (sections * ffp, other)
    return pl.pallas_call(
        functools.partial(_cast_pad_kernel, axis=axis, real_tiles=real, period=period),
        grid=grid,
        in_specs=[pl.BlockSpec((None,) + block, in_map)],
        out_specs=pl.BlockSpec(block, lambda i, j: (i, j)),
        out_shape=jax.ShapeDtypeStruct(out_shape, BF16),
        compiler_params=_params("parallel", "parallel"),
        name="cast_pad",
    )(w)


def _pad_cols(a, n):
    return jnp.pad(a, ((0, 0), (0, n - a.shape[1])))


def _prep_weights(w_in, b_igate, b_fgate, w_mlstm_out, w_fourier_out, w_out, w_up, conv_w, conv_b, w_down,
                  *, heads, vw, fw):
    d, in_w = w_in.shape[1:]
    qk_w = (in_w - 2 * vw - 4 * heads - fw - 2 * d) // 2
    off_i = 2 * qk_w + 2 * vw
    off_fr = off_i + 4 * heads
    w_main = _cast_skip_cols(w_in, off_i, off_fr - off_i)
    w_gate = jnp.pad(w_in[0, :, off_i:off_fr].T, ((0, GATE_ROWS - 4 * heads), (0, 0))).astype(BF16)
    ff = w_down.shape[1]
    ffp = -(-ff // 1024) * 1024
    w_up_p = _cast_pad_halves(w_up, ff, ffp, axis=1)
    cw_p = jnp.concatenate([_pad_cols(conv_w[0][:, :ff], ffp), _pad_cols(conv_w[0][:, ff:], ffp)], axis=1)
    cb_p = jnp.concatenate([jnp.pad(conv_b[0][:ff], (0, ffp - ff)), jnp.pad(conv_b[0][ff:], (0, ffp - ff))])
    w_down_p = _cast_pad_halves(w_down, ff, ffp, axis=0)
    return dict(
        w_main=w_main, w_gate=w_gate, qk_w=qk_w,
        b_ig=b_igate[0].reshape(-1), b_fg=b_fgate[0].reshape(-1),
        w_mo=w_mlstm_out[0].astype(BF16), w_fo=w_fourier_out[0].astype(BF16), w_out=w_out[0].astype(BF16),
        w_up=w_up_p, cw=cw_p, cb=cb_p, w_down=w_down_p, ffp=ffp,
    )


def _trunk(x, wts, norm_mix, mh_norm, b_merge, norm_ffn, norm_final, *, heads, groups):
    batch, seq, d = x.shape
    t = batch * seq
    x2 = x.reshape(t, d)
    vw = wts["w_mo"].shape[0]
    fw = wts["w_fo"].shape[0]
    qk_w = wts["qk_w"]
    dqk, dv, gdim = qk_w // heads, vw // heads, fw // groups
    ooff = 2 * qk_w + vw
    froff = ooff + vw
    gmoff = froff + fw

    p, gates_t = _in_proj(x2, norm_mix[0], wts["w_main"], wts["w_gate"])
    h2 = _mlstm(p, gates_t, wts["b_ig"], wts["b_fg"], batch=batch, seq=seq, heads=heads, dqk=dqk, dv=dv)
    hn = _head_norm(h2, p, mh_norm[0], heads=heads, dv=dv, ooff=ooff)

    cos_c, sin_c = _twiddles(gdim, gdim)
    w_cs = jnp.concatenate([cos_c, sin_c], axis=1).astype(BF16)
    cos_s, sin_s = _twiddles(seq, seq // 2)
    col = lax.broadcasted_iota(jnp.int32, cos_s.shape, 1)
    cos_w = jnp.where(col == 0, 0.5, cos_s).astype(BF16)
    y = _dft_fold(p, w_cs, batch=batch, seq=seq, groups=groups, gdim=gdim, froff=froff)
    fr = _dft_pos(y, p, cos_w, (-sin_s).astype(BF16), w_cs,
                  batch=batch, seq=seq, groups=groups, gdim=gdim, froff=froff)

    merged = _merge(hn, fr, wts["w_mo"], wts["w_fo"], p, b_merge[0], gmoff=gmoff)
    x1 = _proj_res(merged, wts["w_out"], x2, tn=512, a_buffers=2, name="out_proj")

    hf = _ffn_up(x1, norm_ffn[0], wts["w_up"], wts["cw"], wts["cb"], seq=seq)
    ffp = wts["ffp"]
    x3 = _proj_res(hf, wts["w_down"], x1, tn=256, a_buffers=1, name="down_proj")
    return _final_norm(x3, norm_final).reshape(batch, seq, d)


def kernel(x_prompt, x_sample, norm_mix, w_in, b_igate, b_fgate, mh_norm, w_mlstm_out, w_fourier_out, b_merge, w_out, norm_ffn, w_up, conv_w, conv_b, w_down, norm_final):
    heads = b_igate.shape[-1]
    vw, fw = w_mlstm_out.shape[1], w_fourier_out.shape[1]
    wts = _prep_weights(w_in, b_igate, b_fgate, w_mlstm_out, w_fourier_out, w_out, w_up, conv_w, conv_b, w_down,
                        heads=heads, vw=vw, fw=fw)
    run = functools.partial(_trunk, wts=wts, norm_mix=norm_mix, mh_norm=mh_norm, b_merge=b_merge,
                            norm_ffn=norm_ffn, norm_final=norm_final, heads=heads, groups=F_GROUPS)
    return (run(x_prompt), run(x_sample))
```

```python
import functools
import math

import jax
import jax.numpy as jnp
from jax import lax
from jax.experimental import pallas as pl
from jax.experimental.pallas import tpu as pltpu

F32 = jnp.float32
BF16 = jnp.bfloat16

EPS = 1e-6
F_GROUPS = 4
CONV_W = 3
M_CHUNK = 256
M_HEADS_PER_STEP = 8

V7X_VMEM_BYTES = 64 * 1024 * 1024
VMEM_LIMIT = V7X_VMEM_BYTES - 8 * 1024 * 1024
LANES = 128
BF16_SUBLANES = 16
GATE_ROWS = 128
NORM_SLAB = 128


def _tile(n, pref):
    return pref if n % pref == 0 else n


def _params(*sem):
    return pltpu.CompilerParams(dimension_semantics=sem, vmem_limit_bytes=VMEM_LIMIT)


def _norm_rows(x_hbm, g_ref, xbuf, sem, xn_ref, row0, *, n_rows, dst0):
    slab = xbuf.shape[1]
    n_slabs = n_rows // slab

    def copy(r, slot):
        src = x_hbm.at[pl.ds(pl.multiple_of(row0 + r * slab, BF16_SUBLANES), slab), :]
        return pltpu.make_async_copy(src, xbuf.at[slot], sem.at[slot])

    copy(0, 0).start()

    def body(r, carry):
        slot = r % 2

        @pl.when(r + 1 < n_slabs)
        def _():
            copy(r + 1, 1 - slot).start()

        copy(r, slot).wait()
        x = xbuf[slot]
        ms = jnp.mean(x * x, axis=-1, keepdims=True)
        rows = pl.ds(pl.multiple_of(dst0 + r * slab, BF16_SUBLANES), slab)
        xn_ref[rows, :] = (x * lax.rsqrt(ms + EPS) * g_ref[...]).astype(BF16)
        return carry

    lax.fori_loop(0, n_slabs, body, 0)


def _in_proj_kernel(x_hbm, g_ref, w_ref, wg_ref, o_ref, gt_ref, xn_ref, xbuf, sem, *, tm):
    @pl.when(pl.program_id(1) == 0)
    def _():
        _norm_rows(x_hbm, g_ref, xbuf, sem, xn_ref, pl.program_id(0) * tm, n_rows=tm, dst0=0)
        gt_ref[...] = lax.dot_general(wg_ref[...], xn_ref[...], (((1,), (1,)), ((), ())),
                                      preferred_element_type=F32)

    o_ref[...] = jnp.dot(xn_ref[...], w_ref[...], preferred_element_type=F32).astype(o_ref.dtype)


def _in_proj(x, gain, w, w_gate):
    t, d = x.shape
    n = w.shape[1]
    tm, tn = _tile(t, 1024), _tile(n, 1024)
    slab = min(NORM_SLAB, tm)
    return pl.pallas_call(
        functools.partial(_in_proj_kernel, tm=tm),
        grid=(t // tm, n // tn),
        in_specs=[
            pl.BlockSpec(memory_space=pl.ANY),
            pl.BlockSpec((1, d), lambda i, j: (0, 0)),
            pl.BlockSpec((d, tn), lambda i, j: (0, j)),
            pl.BlockSpec((GATE_ROWS, d), lambda i, j: (0, 0)),
        ],
        out_specs=[pl.BlockSpec((tm, tn), lambda i, j: (i, j)),
                   pl.BlockSpec((GATE_ROWS, tm), lambda i, j: (0, i))],
        out_shape=[jax.ShapeDtypeStruct((t, n), BF16), jax.ShapeDtypeStruct((GATE_ROWS, t), F32)],
        scratch_shapes=[pltpu.VMEM((tm, d), BF16), pltpu.VMEM((2, slab, d), F32),
                        pltpu.SemaphoreType.DMA((2,))],
        compiler_params=_params("parallel", "arbitrary"),
        name="in_proj",
    )(x, gain.reshape(1, d), w, w_gate)


HALO = BF16_SUBLANES
SUBLANES = 8
EPI_ROWS = 32


def _ffn_up_kernel(x_hbm, g_ref, wa_ref, wv_ref, cwa_ref, cwv_ref, cba_ref, cbv_ref, o_ref,
                   xn_ref, xbuf, hbuf, sem, hsem, ra0, rv0, ra1, rv1, *, tm, nj, n_tiles, seq, t_rows):
    s = pl.program_id(0)
    i = s // nj
    r0 = i * tm

    @pl.when(s == 0)
    def _():
        ra1[...] = jnp.zeros_like(ra1)
        rv1[...] = jnp.zeros_like(rv1)

    @pl.when(jnp.logical_and(s % nj == 0, s < n_tiles))
    def _():
        tops = (jnp.maximum(r0 - HALO, 0), jnp.minimum(r0 + tm, t_rows - HALO))
        halo = [pltpu.make_async_copy(x_hbm.at[pl.ds(pl.multiple_of(tops[e], HALO), HALO), :],
                                      hbuf.at[e], hsem.at[e]) for e in (0, 1)]
        for cp in halo:
            cp.start()
        _norm_rows(x_hbm, g_ref, xbuf, sem, xn_ref, r0, n_rows=tm, dst0=HALO)
        for e, cp in enumerate(halo):
            cp.wait()
            x = hbuf[e]
            ms = jnp.mean(x * x, axis=-1, keepdims=True)
            dst = 0 if e == 0 else HALO + tm
            xn_ref[dst:dst + HALO, :] = (x * lax.rsqrt(ms + EPS) * g_ref[...]).astype(BF16)

    at_start = (r0 % seq) == 0
    at_end = ((r0 + tm) % seq) == 0

    def step(wr_a, wr_v, rd_a, rd_v):
        def conv(rd, cw_ref, cb_ref, r):
            cw = cw_ref[...]
            rows = EPI_ROWS + 2 * SUBLANES
            win = rd[HALO + r - SUBLANES:HALO + r + EPI_ROWS + SUBLANES, :]
            up = pltpu.roll(win, 1, 0)[SUBLANES:SUBLANES + EPI_ROWS]
            dn = pltpu.roll(win, rows - 1, 0)[SUBLANES:SUBLANES + EPI_ROWS]
            return cw[0:1] * up + cw[1:2] * win[SUBLANES:SUBLANES + EPI_ROWS] + cw[2:3] * dn + cb_ref[...]

        for r in range(0, tm, EPI_ROWS):
            a = conv(rd_a, cwa_ref, cba_ref, r)
            val = conv(rd_v, cwv_ref, cbv_ref, r)
            o_ref[r:r + EPI_ROWS, :] = (0.5 * a * (1.0 + lax.erf(a * math.sqrt(0.5))) * val).astype(BF16)

        xn = xn_ref[...]
        wr_a[...] = jnp.dot(xn, wa_ref[...], preferred_element_type=F32)
        wr_v[...] = jnp.dot(xn, wv_ref[...], preferred_element_type=F32)

        zero_row = jnp.zeros((1, wr_a.shape[1]), F32)

        @pl.when(at_start)
        def _():
            wr_a[HALO - 1:HALO, :] = zero_row
            wr_v[HALO - 1:HALO, :] = zero_row

        @pl.when(at_end)
        def _():
            wr_a[HALO + tm:HALO + tm + 1, :] = zero_row
            wr_v[HALO + tm:HALO + tm + 1, :] = zero_row

    @pl.when(s % 2 == 0)
    def _():
        step(ra0, rv0, ra1, rv1)

    @pl.when(s % 2 == 1)
    def _():
        step(ra1, rv1, ra0, rv0)


def _ffn_up(x, gain, w, cw, cb, *, seq):
    t, d = x.shape
    ff = w.shape[1] // 2
    tm, tn = _tile(seq, 1024), _tile(ff, 512)
    nj = ff // tn
    n_tiles = (t // tm) * nj
    slab = min(NORM_SLAB, tm)
    cb2 = cb.reshape(1, 2 * ff)

    def mm_tile(s):
        return jnp.minimum(s, n_tiles - 1) % nj

    def ep_tile(s):
        return jnp.maximum(s - 1, 0)

    raw = pltpu.VMEM((tm + 2 * HALO, tn), F32)
    return pl.pallas_call(
        functools.partial(_ffn_up_kernel, tm=tm, nj=nj, n_tiles=n_tiles, seq=seq, t_rows=t),
        grid=(n_tiles + 1,),
        in_specs=[
            pl.BlockSpec(memory_space=pl.ANY),
            pl.BlockSpec((1, d), lambda s: (0, 0)),
            pl.BlockSpec((d, tn), lambda s: (0, mm_tile(s))),
            pl.BlockSpec((d, tn), lambda s: (0, nj + mm_tile(s))),
            pl.BlockSpec((CONV_W, tn), lambda s: (0, ep_tile(s) % nj)),
            pl.BlockSpec((CONV_W, tn), lambda s: (0, nj + ep_tile(s) % nj)),
            pl.BlockSpec((1, tn), lambda s: (0, ep_tile(s) % nj)),
            pl.BlockSpec((1, tn), lambda s: (0, nj + ep_tile(s) % nj)),
        ],
        out_specs=pl.BlockSpec((tm, tn), lambda s: (ep_tile(s) // nj, ep_tile(s) % nj)),
        out_shape=jax.ShapeDtypeStruct((t, ff), BF16),
        scratch_shapes=[pltpu.VMEM((tm + 2 * HALO, d), BF16), pltpu.VMEM((2, slab, d), F32),
                        pltpu.VMEM((2, HALO, d), F32), pltpu.SemaphoreType.DMA((2,)),
                        pltpu.SemaphoreType.DMA((2,)), raw, raw, raw, raw],
        compiler_params=_params("arbitrary"),
        name="ffn_up",
    )(x, gain.reshape(1, d), w, w, cw, cw, cb2, cb2)


def _mlstm_kernel(big_ref, bfg_ref, q_ref, k_ref, v_ref, gi_ref, gf_ref, o_ref, ct_ref, m_ref,
                  *, heads, hps, dqk, dv, chunk):
    d = pl.program_id(0)
    hg = pl.program_id(2)
    c = pl.program_id(3)

    @pl.when(c == 0)
    def _():
        ct_ref[...] = jnp.zeros_like(ct_ref)
        m_ref[...] = jnp.zeros_like(m_ref)

    sign = 1 - 2 * d
    row = lax.broadcasted_iota(jnp.int32, (chunk, chunk), 0)
    col = lax.broadcasted_iota(jnp.int32, (chunk, chunk), 1)
    before = ((row - col) * sign <= 0).astype(BF16)
    mask = ((col - row) * sign <= 0)
    diag = row == col
    prow = lax.broadcasted_iota(jnp.int32, (BF16_SUBLANES, chunk), 0)
    ones_col = (lax.broadcasted_iota(jnp.int32, (chunk, LANES), 1) == 0).astype(BF16)
    kscale = dqk ** -0.5

    for u in range(hps):
        h = hg * hps + u
        gidx = d * heads + h
        i_row = gi_ref[pl.ds(h, 1), :] + big_ref[gidx]
        xf = gf_ref[pl.ds(h, 1), :] + bfg_ref[gidx]
        lf_row = jnp.minimum(xf, 0.0) - jnp.log1p(jnp.exp(-jnp.abs(xf)))

        hi = lf_row.astype(BF16).astype(F32)
        r1 = lf_row - hi
        mid = r1.astype(BF16).astype(F32)
        lo = r1 - mid
        parts = jnp.where(prow == 0, hi, jnp.where(prow == 1, mid, jnp.where(prow == 2, lo, 0.0)))
        b3 = jnp.dot(parts.astype(BF16), before, preferred_element_type=F32)
        b_row = b3[0:1] + b3[1:2] + b3[2:3]
        g_tot = jnp.sum(lf_row, axis=1, keepdims=True)

        r_row = i_row - b_row
        m_prev = m_ref[u, 0:1, 0:1]
        rm = jnp.where(mask, r_row, -jnp.inf)
        m_col = jnp.maximum(jnp.max(rm, axis=1, keepdims=True), m_prev)
        m_all = jnp.maximum(jnp.max(r_row, axis=1, keepdims=True), m_prev)
        dmat = jnp.exp(rm - m_col)

        q = q_ref[:, u * dqk:(u + 1) * dqk]
        k = k_ref[:, u * dqk:(u + 1) * dqk]
        v_aug = jnp.concatenate([v_ref[:, u * dv:(u + 1) * dv], ones_col], axis=1)

        qk = lax.dot_general(q, k, (((1,), (1,)), ((), ())), preferred_element_type=F32)
        s = (qk * kscale * dmat).astype(BF16)
        ct = ct_ref[u]
        inter = jnp.dot(q, ct.astype(BF16), preferred_element_type=F32)
        sc = jnp.exp(m_prev - m_col) * kscale
        numden = sc * inter + jnp.dot(s, v_aug, preferred_element_type=F32)
        num = numden[:, :dv]
        den = numden[:, dv:dv + 1]
        b_col = jnp.sum(jnp.where(diag, b_row, 0.0), axis=1, keepdims=True)
        floor = jnp.exp(-(b_col + m_col))
        o_ref[:, u * dv:(u + 1) * dv] = (num * (1.0 / jnp.maximum(jnp.abs(den), floor))).astype(BF16)

        wa_row = jnp.exp(r_row - m_all)
        decay = jnp.exp(m_prev - m_all)
        ktw = (k.astype(F32).T * wa_row).astype(BF16)
        ct_ref[u] = decay * ct + jnp.dot(ktw, v_aug, preferred_element_type=F32)
        m_ref[u] = jnp.broadcast_to(g_tot + m_all, m_ref.shape[1:])


def _mlstm(p, gates_t, b_ig, b_fg, *, batch, seq, heads, dqk, dv):
    t = batch * seq
    chunk = _tile(seq, M_CHUNK)
    nc = seq // chunk
    hps = M_HEADS_PER_STEP
    ng = heads // hps
    koff = ng
    voff = 2 * heads * dqk // (hps * dv)

    def rows(d, b, c):
        return b * nc + c + d * (nc - 1 - 2 * c)

    smem = pl.BlockSpec(memory_space=pltpu.SMEM)
    return pl.pallas_call(
        functools.partial(_mlstm_kernel, heads=heads, hps=hps, dqk=dqk, dv=dv, chunk=chunk),
        grid=(2, batch, ng, nc),
        in_specs=[
            smem, smem,
            pl.BlockSpec((chunk, hps * dqk), lambda d, b, g, c: (rows(d, b, c), g)),
            pl.BlockSpec((chunk, hps * dqk), lambda d, b, g, c: (rows(d, b, c), koff + g)),
            pl.BlockSpec((chunk, hps * dv), lambda d, b, g, c: (rows(d, b, c), voff + g)),
            pl.BlockSpec((heads, chunk), lambda d, b, g, c: (d, rows(d, b, c))),
            pl.BlockSpec((heads, chunk), lambda d, b, g, c: (2 + d, rows(d, b, c))),
        ],
        out_specs=pl.BlockSpec((None, chunk, hps * dv), lambda d, b, g, c: (d, rows(d, b, c), g)),
        out_shape=jax.ShapeDtypeStruct((2, t, heads * dv), BF16),
        scratch_shapes=[pltpu.VMEM((hps, dqk, dv + LANES), F32), pltpu.VMEM((hps, 8, LANES), F32)],
        compiler_params=_params("parallel", "parallel", "parallel", "arbitrary"),
        name="mlstm",
    )(b_ig, b_fg, p, p, p, gates_t, gates_t)


def _head_norm_kernel(h_ref, og_ref, g_ref, o_ref):
    hs = h_ref[0].astype(F32) + h_ref[1].astype(F32)
    ms = jnp.mean(hs * hs, axis=-1, keepdims=True)
    y = hs * lax.rsqrt(ms + EPS) * g_ref[...]
    o_ref[...] = (y * jax.nn.sigmoid(og_ref[...].astype(F32))).astype(BF16)


def _head_norm(h2, p, gain, *, heads, dv, ooff):
    _, t, vw = h2.shape
    tm = _tile(t, 1024)
    return pl.pallas_call(
        _head_norm_kernel,
        grid=(t // tm, heads),
        in_specs=[
            pl.BlockSpec((2, tm, dv), lambda i, h: (0, i, h)),
            pl.BlockSpec((tm, dv), lambda i, h: (i, ooff // dv + h)),
            pl.BlockSpec((1, dv), lambda i, h: (0, h)),
        ],
        out_specs=pl.BlockSpec((tm, dv), lambda i, h: (i, h)),
        out_shape=jax.ShapeDtypeStruct((t, vw), BF16),
        compiler_params=_params("parallel", "parallel"),
        name="head_norm",
    )(h2, p, gain.reshape(1, vw))


def _twiddles(n, cols):
    r = math.gcd(n, 64)
    k = jnp.arange(cols, dtype=jnp.int32)[None, :]

    def cos_sin(j):
        ang = ((j[:, None] * k) % n).astype(F32) * (2.0 * math.pi / n)
        return jnp.cos(ang), jnp.sin(ang)

    ac, asn = cos_sin(jnp.arange(n // r, dtype=jnp.int32) * r)
    bc, bsn = cos_sin(jnp.arange(r, dtype=jnp.int32))
    cos = (ac[:, None, :] * bc[None] - asn[:, None, :] * bsn[None]).reshape(n, cols)
    sin = (asn[:, None, :] * bc[None] + ac[:, None, :] * bsn[None]).reshape(n, cols)
    return cos, sin


def _dft_fold_kernel(xd_ref, xa_ref, xb_ref, w_ref, o_ref, *, tm, gdim):
    u = lax.broadcasted_iota(jnp.int32, (tm, 2 * tm), 0)
    v = lax.broadcasted_iota(jnp.int32, (tm, 2 * tm), 1)
    pick = (v == tm - u).astype(BF16)
    pair = jnp.concatenate([xa_ref[...], xb_ref[...]], axis=0)
    xr = jnp.dot(pick, pair, preferred_element_type=F32)
    xd = xd_ref[...].astype(F32)
    w = w_ref[...]
    o_ref[:, :gdim] = jnp.dot((xd + xr).astype(BF16), w[:, :gdim], preferred_element_type=F32).astype(BF16)
    o_ref[:, gdim:] = jnp.dot((xd - xr).astype(BF16), w[:, gdim:], preferred_element_type=F32).astype(BF16)


def _dft_fold(p, w_cs, *, batch, seq, groups, gdim, froff):
    half = seq // 2
    tm = _tile(half, 256)
    nt, nh = seq // tm, half // tm
    cb = froff // gdim
    return pl.pallas_call(
        functools.partial(_dft_fold_kernel, tm=tm, gdim=gdim),
        grid=(batch, nh, groups),
        in_specs=[
            pl.BlockSpec((tm, gdim), lambda b, i, g: (b * nt + i, cb + g)),
            pl.BlockSpec((tm, gdim), lambda b, i, g: (b * nt + nt - 1 - i, cb + g)),
            pl.BlockSpec((tm, gdim), lambda b, i, g: (b * nt + (nt - i) % nt, cb + g)),
            pl.BlockSpec((gdim, 2 * gdim), lambda b, i, g: (0, 0)),
        ],
        out_specs=pl.BlockSpec((tm, 2 * gdim), lambda b, i, g: (b * nh + i, g)),
        out_shape=jax.ShapeDtypeStruct((batch * half, 2 * groups * gdim), BF16),
        compiler_params=_params("parallel", "parallel", "parallel"),
        name="dft_fold",
    )(p, p, p, w_cs)


def _dft_pos_kernel(c_ref, s_ref, e_ref, om_ref, xh_ref, wc_ref, o_ref, yh_ref, *, scale, tk):
    i = pl.program_id(2)

    @pl.when(i == 0)
    def _():
        yh_ref[...] = jnp.dot(xh_ref[...], wc_ref[...], preferred_element_type=F32)

    acc = jnp.dot(c_ref[...], e_ref[...], preferred_element_type=F32)
    acc = acc + jnp.dot(s_ref[...], om_ref[...], preferred_element_type=F32)
    k1 = i * tk + lax.broadcasted_iota(jnp.int32, (tk, 1), 0)
    sgn = (1 - 2 * (k1 & 1)).astype(F32)
    o_ref[...] = ((acc + sgn * yh_ref[0:1, :]) * scale).astype(BF16)


def _dft_pos(y, p, cos_w, nsin, w_cs, *, batch, seq, groups, gdim, froff):
    t = batch * seq
    half = seq // 2
    tk = _tile(seq, 1024)
    nk = seq // tk
    cb = froff // gdim
    scale = 1.0 / math.sqrt(seq * gdim)
    return pl.pallas_call(
        functools.partial(_dft_pos_kernel, scale=scale, tk=tk),
        grid=(batch, groups, nk),
        in_specs=[
            pl.BlockSpec((tk, half), lambda b, g, i: (i, 0)),
            pl.BlockSpec((tk, half), lambda b, g, i: (i, 0)),
            pl.BlockSpec((half, gdim), lambda b, g, i: (b, 2 * g), pipeline_mode=pl.Buffered(1)),
            pl.BlockSpec((half, gdim), lambda b, g, i: (b, 2 * g + 1), pipeline_mode=pl.Buffered(1)),
            pl.BlockSpec((BF16_SUBLANES, gdim),
                         lambda b, g, i: ((b * seq + half) // BF16_SUBLANES, cb + g)),
            pl.BlockSpec((gdim, gdim), lambda b, g, i: (0, 0)),
        ],
        out_specs=pl.BlockSpec((tk, gdim), lambda b, g, i: (b * nk + i, g)),
        out_shape=jax.ShapeDtypeStruct((t, groups * gdim), BF16),
        scratch_shapes=[pltpu.VMEM((BF16_SUBLANES, gdim), F32)],
        compiler_params=_params("parallel", "parallel", "arbitrary"),
        name="dft_pos",
    )(cos_w, nsin, y, y, p, w_cs)


def _merge_kernel(hn_ref, fr_ref, wmo_ref, wfo_ref, gm_ref, gf_ref, bm_ref, bf_ref, o_ref):
    hm = jnp.dot(hn_ref[...], wmo_ref[...], preferred_element_type=F32)
    hf = jnp.dot(fr_ref[...], wfo_ref[...], preferred_element_type=F32)
    g1 = jax.nn.sigmoid(gm_ref[...].astype(F32) + bm_ref[...])
    g2 = jax.nn.sigmoid(gf_ref[...].astype(F32) + bf_ref[...])
    o_ref[...] = (g1 * hm + g2 * hf).astype(BF16)


def _merge(hn, fr, w_mo, w_fo, p, b_merge, *, gmoff):
    t, vw = hn.shape
    fw = fr.shape[1]
    d = w_mo.shape[1]
    tm, tn = _tile(t, 1024), _tile(d, 512)
    nj = d // tn
    bm = b_merge.reshape(1, 2 * d)
    return pl.pallas_call(
        _merge_kernel,
        grid=(t // tm, nj),
        in_specs=[
            pl.BlockSpec((tm, vw), lambda i, j: (i, 0), pipeline_mode=pl.Buffered(1)),
            pl.BlockSpec((tm, fw), lambda i, j: (i, 0), pipeline_mode=pl.Buffered(1)),
            pl.BlockSpec((vw, tn), lambda i, j: (0, j)),
            pl.BlockSpec((fw, tn), lambda i, j: (0, j)),
            pl.BlockSpec((tm, tn), lambda i, j: (i, gmoff // tn + j)),
            pl.BlockSpec((tm, tn), lambda i, j: (i, gmoff // tn + nj + j)),
            pl.BlockSpec((1, tn), lambda i, j: (0, j)),
            pl.BlockSpec((1, tn), lambda i, j: (0, nj + j)),
        ],
        out_specs=pl.BlockSpec((tm, tn), lambda i, j: (i, j)),
        out_shape=jax.ShapeDtypeStruct((t, d), BF16),
        compiler_params=_params("parallel", "arbitrary"),
        name="merge",
    )(hn, fr, w_mo, w_fo, p, p, bm, bm)


def _proj_res_kernel(a_ref, w_ref, r_ref, o_ref):
    o_ref[...] = r_ref[...] + jnp.dot(a_ref[...], w_ref[...], preferred_element_type=F32)


def _proj_res(a, w, res, *, tn, a_buffers, name):
    t, kdim = a.shape
    n = w.shape[1]
    tm, tn = _tile(t, 1024), _tile(n, tn)
    return pl.pallas_call(
        _proj_res_kernel,
        grid=(t // tm, n // tn),
        in_specs=[
            pl.BlockSpec((tm, kdim), lambda i, j: (i, 0), pipeline_mode=pl.Buffered(a_buffers)),
            pl.BlockSpec((kdim, tn), lambda i, j: (0, j)),
            pl.BlockSpec((tm, tn), lambda i, j: (i, j)),
        ],
        out_specs=pl.BlockSpec((tm, tn), lambda i, j: (i, j)),
        out_shape=jax.ShapeDtypeStruct((t, n), F32),
        compiler_params=_params("parallel", "arbitrary"),
        name=name,
    )(a, w, res)


def _final_norm_kernel(x_ref, g_ref, o_ref):
    x = x_ref[...]
    ms = jnp.mean(x * x, axis=-1, keepdims=True)
    o_ref[...] = x * lax.rsqrt(ms + EPS) * g_ref[...]


def _final_norm(x, gain):
    t, d = x.shape
    tm = _tile(t, 256)
    return pl.pallas_call(
        _final_norm_kernel,
        grid=(t // tm,),
        in_specs=[pl.BlockSpec((tm, d), lambda i: (i, 0)), pl.BlockSpec((1, d), lambda i: (0, 0))],
        out_specs=pl.BlockSpec((tm, d), lambda i: (i, 0)),
        out_shape=jax.ShapeDtypeStruct((t, d), F32),
        compiler_params=_params("parallel"),
        name="final_norm",
    )(x, gain.reshape(1, d))


def _cast_skip_kernel(a_ref, b_ref, o_ref, *, first_shifted, shift):
    a = a_ref[...]
    shifted = jnp.concatenate([a[:, shift:], b_ref[:, :shift]], axis=1)
    o_ref[...] = jnp.where(pl.program_id(1) >= first_shifted, shifted, a).astype(BF16)


def _cast_skip_cols(w, skip_at, skip):
    _, rows, cols = w.shape
    n_out = cols - skip
    tc = next(c for c in (1024, 512, 256, LANES) if skip_at % c == 0 and n_out % c == 0)
    tr = _tile(rows, 1024)
    per = tc // LANES
    return pl.pallas_call(
        functools.partial(_cast_skip_kernel, first_shifted=skip_at // tc, shift=skip),
        grid=(rows // tr, n_out // tc),
        in_specs=[pl.BlockSpec((None, tr, tc), lambda i, j: (0, i, j)),
                  pl.BlockSpec((None, tr, LANES), lambda i, j: (0, i, (j + 1) * per))],
        out_specs=pl.BlockSpec((tr, tc), lambda i, j: (i, j)),
        out_shape=jax.ShapeDtypeStruct((rows, n_out), BF16),
        compiler_params=_params("parallel", "parallel"),
        name="cast_skip_cols",
    )(w, w)


def _cast_pad_kernel(a_ref, o_ref, *, axis, real_tiles, period):
    is_pad = (pl.program_id(axis) % period) >= real_tiles
    o_ref[...] = jnp.where(is_pad, 0.0, a_ref[...]).astype(BF16)


def _cast_pad_halves(w, ff, ffp, *, axis):
    tile = 256
    dims = w.shape[1:]
    assert ff % tile == 0 and ffp % tile == 0 and dims[axis] % ff == 0
    real, period = ff // tile, ffp // tile
    sections = dims[axis] // ff
    other = dims[1 - axis]
    to = _tile(other, 4096)

    def src(t):
        return (t // period) * real + jnp.minimum(t % period, real - 1)

    if axis == 1:
        block, grid = (to, tile), (other // to, sections * period)
        in_map = lambda i, j: (0, i, src(j))
        out_shape = (other, sections * ffp)
    else:
        block, grid = (tile, to), (sections * period, other // to)
        in_map = lambda i, j: (0, src(i), j)
        out_shape = (sections * ffp, other)
    return pl.pallas_call(
        functools.partial(_cast_pad_kernel, axis=axis, real_tiles=real, period=period),
        grid=grid,
        in_specs=[pl.BlockSpec((None,) + block, in_map)],
        out_specs=pl.BlockSpec(block, lambda i, j: (i, j)),
        out_shape=jax.ShapeDtypeStruct(out_shape, BF16),
        compiler_params=_params("parallel", "parallel"),
        name="cast_pad",
    )(w)


def _pad_cols(a, n):
    return jnp.pad(a, ((0, 0), (0, n - a.shape[1])))


def _prep_weights(w_in, b_igate, b_fgate, w_mlstm_out, w_fourier_out, w_out, w_up, conv_w, conv_b, w_down,
                  *, heads, vw, fw):
    d, in_w = w_in.shape[1:]
    qk_w = (in_w - 2 * vw - 4 * heads - fw - 2 * d) // 2
    off_i = 2 * qk_w + 2 * vw
    off_fr = off_i + 4 * heads
    w_main = _cast_skip_cols(w_in, off_i, off_fr - off_i)
    w_gate = jnp.pad(w_in[0, :, off_i:off_fr].T, ((0, GATE_ROWS - 4 * heads), (0, 0))).astype(BF16)
    ff = w_down.shape[1]
    ffp = -(-ff // 1024) * 1024
    w_up_p = _cast_pad_halves(w_up, ff, ffp, axis=1)
    cw_p = jnp.concatenate([_pad_cols(conv_w[0][:, :ff], ffp), _pad_cols(conv_w[0][:, ff:], ffp)], axis=1)
    cb_p = jnp.concatenate([jnp.pad(conv_b[0][:ff], (0, ffp - ff)), jnp.pad(conv_b[0][ff:], (0, ffp - ff))])
    w_down_p = _cast_pad_halves(w_down, ff, ffp, axis=0)
    return dict(
        w_main=w_main, w_gate=w_gate, qk_w=qk_w,
        b_ig=b_igate[0].reshape(-1), b_fg=b_fgate[0].reshape(-1),
        w_mo=w_mlstm_out[0].astype(BF16), w_fo=w_fourier_out[0].astype(BF16), w_out=w_out[0].astype(BF16),
        w_up=w_up_p, cw=cw_p, cb=cb_p, w_down=w_down_p, ffp=ffp,
    )


def _trunk(x, wts, norm_mix, mh_norm, b_merge, norm_ffn, norm_final, *, heads, groups):
    batch, seq, d = x.shape
    t = batch * seq
    x2 = x.reshape(t, d)
    vw = wts["w_mo"].shape[0]
    fw = wts["w_fo"].shape[0]
    qk_w = wts["qk_w"]
    dqk, dv, gdim = qk_w // heads, vw // heads, fw // groups
    ooff = 2 * qk_w + vw
    froff = ooff + vw
    gmoff = froff + fw

    p, gates_t = _in_proj(x2, norm_mix[0], wts["w_main"], wts["w_gate"])
    h2 = _mlstm(p, gates_t, wts["b_ig"], wts["b_fg"], batch=batch, seq=seq, heads=heads, dqk=dqk, dv=dv)
    hn = _head_norm(h2, p, mh_norm[0], heads=heads, dv=dv, ooff=ooff)

    cos_c, sin_c = _twiddles(gdim, gdim)
    w_cs = jnp.concatenate([cos_c, sin_c], axis=1).astype(BF16)
    cos_s, sin_s = _twiddles(seq, seq // 2)
    col = lax.broadcasted_iota(jnp.int32, cos_s.shape, 1)
    cos_w = jnp.where(col == 0, 0.5, cos_s).astype(BF16)
    y = _dft_fold(p, w_cs, batch=batch, seq=seq, groups=groups, gdim=gdim, froff=froff)
    fr = _dft_pos(y, p, cos_w, (-sin_s).astype(BF16), w_cs,
                  batch=batch, seq=seq, groups=groups, gdim=gdim, froff=froff)

    merged = _merge(hn, fr, wts["w_mo"], wts["w_fo"], p, b_merge[0], gmoff=gmoff)
    x1 = _proj_res(merged, wts["w_out"], x2, tn=512, a_buffers=2, name="out_proj")

    hf = _ffn_up(x1, norm_ffn[0], wts["w_up"], wts["cw"], wts["cb"], seq=seq)
    ffp = wts["ffp"]
    x3 = _proj_res(hf, wts["w_down"], x1, tn=256, a_buffers=1, name="down_proj")
    return _final_norm(x3, norm_final).reshape(batch, seq, d)


def kernel(x_prompt, x_sample, norm_mix, w_in, b_igate, b_fgate, mh_norm, w_mlstm_out, w_fourier_out, b_merge, w_out, norm_ffn, w_up, conv_w, conv_b, w_down, norm_final):
    heads = b_igate.shape[-1]
    vw, fw = w_mlstm_out.shape[1], w_fourier_out.shape[1]
    wts = _prep_weights(w_in, b_igate, b_fgate, w_mlstm_out, w_fourier_out, w_out, w_up, conv_w, conv_b, w_down,
                        heads=heads, vw=vw, fw=fw)
    run = functools.partial(_trunk, wts=wts, norm_mix=norm_mix, mh_norm=mh_norm, b_merge=b_merge,
                            norm_ffn=norm_ffn, norm_final=norm_final, heads=heads, groups=F_GROUPS)
    return (run(x_prompt), run(x_sample))
```

```python
import functools
import math

import jax
import jax.numpy as jnp
from jax import lax
from jax.experimental import pallas as pl
from jax.experimental.pallas import tpu as pltpu

F32 = jnp.float32
BF16 = jnp.bfloat16

EPS = 1e-6
F_GROUPS = 4
CONV_W = 3
M_CHUNK = 256
M_HEADS_PER_STEP = 8

V7X_VMEM_BYTES = 64 * 1024 * 1024
VMEM_LIMIT = V7X_VMEM_BYTES - 8 * 1024 * 1024
LANES = 128
BF16_SUBLANES = 16
GATE_ROWS = 128
NORM_SLAB = 128
DFT_DOUBLE_BUFFER_ELEMS = 2 * 1024 * 1024


def _tile(n, pref):
    return pref if n % pref == 0 else n


def _params(*sem):
    return pltpu.CompilerParams(dimension_semantics=sem, vmem_limit_bytes=VMEM_LIMIT)


def _norm_rows(x_hbm, g_ref, xbuf, sem, xn_ref, row0, *, n_rows, dst0):
    slab = xbuf.shape[1]
    n_slabs = n_rows // slab

    def copy(r, slot):
        src = x_hbm.at[pl.ds(pl.multiple_of(row0 + r * slab, BF16_SUBLANES), slab), :]
        return pltpu.make_async_copy(src, xbuf.at[slot], sem.at[slot])

    copy(0, 0).start()

    def body(r, carry):
        slot = r % 2

        @pl.when(r + 1 < n_slabs)
        def _():
            copy(r + 1, 1 - slot).start()

        copy(r, slot).wait()
        x = xbuf[slot]
        ms = jnp.mean(x * x, axis=-1, keepdims=True)
        rows = pl.ds(pl.multiple_of(dst0 + r * slab, BF16_SUBLANES), slab)
        xn_ref[rows, :] = (x * lax.rsqrt(ms + EPS) * g_ref[...]).astype(BF16)
        return carry

    lax.fori_loop(0, n_slabs, body, 0)


def _in_proj_kernel(x_hbm, g_ref, w_ref, wg_ref, o_ref, gt_ref, xn_ref, xbuf, sem, *, tm):
    @pl.when(pl.program_id(1) == 0)
    def _():
        _norm_rows(x_hbm, g_ref, xbuf, sem, xn_ref, pl.program_id(0) * tm, n_rows=tm, dst0=0)
        gt_ref[...] = lax.dot_general(wg_ref[...], xn_ref[...], (((1,), (1,)), ((), ())),
                                      preferred_element_type=F32)

    o_ref[...] = jnp.dot(xn_ref[...], w_ref[...], preferred_element_type=F32).astype(o_ref.dtype)


def _in_proj(x, gain, w, w_gate):
    t, d = x.shape
    n = w.shape[1]
    tm, tn = _tile(t, 1024), _tile(n, 1024)
    slab = min(NORM_SLAB, tm)
    return pl.pallas_call(
        functools.partial(_in_proj_kernel, tm=tm),
        grid=(t // tm, n // tn),
        in_specs=[
            pl.BlockSpec(memory_space=pl.ANY),
            pl.BlockSpec((1, d), lambda i, j: (0, 0)),
            pl.BlockSpec((d, tn), lambda i, j: (0, j)),
            pl.BlockSpec((GATE_ROWS, d), lambda i, j: (0, 0)),
        ],
        out_specs=[pl.BlockSpec((tm, tn), lambda i, j: (i, j)),
                   pl.BlockSpec((GATE_ROWS, tm), lambda i, j: (0, i))],
        out_shape=[jax.ShapeDtypeStruct((t, n), BF16), jax.ShapeDtypeStruct((GATE_ROWS, t), F32)],
        scratch_shapes=[pltpu.VMEM((tm, d), BF16), pltpu.VMEM((2, slab, d), F32),
                        pltpu.SemaphoreType.DMA((2,))],
        compiler_params=_params("parallel", "arbitrary"),
        name="in_proj",
    )(x, gain.reshape(1, d), w, w_gate)


HALO = BF16_SUBLANES
SUBLANES = 8
EPI_ROWS = 32


def _ffn_up_kernel(x_hbm, g_ref, wa_ref, wv_ref, cwa_ref, cwv_ref, cba_ref, cbv_ref, o_ref,
                   xn_ref, xbuf, hbuf, sem, hsem, ra0, rv0, ra1, rv1, *, tm, nj, n_tiles, seq, t_rows):
    s = pl.program_id(0)
    i = s // nj
    r0 = i * tm

    @pl.when(s == 0)
    def _():
        ra1[...] = jnp.zeros_like(ra1)
        rv1[...] = jnp.zeros_like(rv1)

    @pl.when(jnp.logical_and(s % nj == 0, s < n_tiles))
    def _():
        tops = (jnp.maximum(r0 - HALO, 0), jnp.minimum(r0 + tm, t_rows - HALO))
        halo = [pltpu.make_async_copy(x_hbm.at[pl.ds(pl.multiple_of(tops[e], HALO), HALO), :],
                                      hbuf.at[e], hsem.at[e]) for e in (0, 1)]
        for cp in halo:
            cp.start()
        _norm_rows(x_hbm, g_ref, xbuf, sem, xn_ref, r0, n_rows=tm, dst0=HALO)
        for e, cp in enumerate(halo):
            cp.wait()
            x = hbuf[e]
            ms = jnp.mean(x * x, axis=-1, keepdims=True)
            dst = 0 if e == 0 else HALO + tm
            xn_ref[dst:dst + HALO, :] = (x * lax.rsqrt(ms + EPS) * g_ref[...]).astype(BF16)

    at_start = (r0 % seq) == 0
    at_end = ((r0 + tm) % seq) == 0

    def step(wr_a, wr_v, rd_a, rd_v):
        def conv(rd, cw_ref, cb_ref, r):
            cw = cw_ref[...]
            rows = EPI_ROWS + 2 * SUBLANES
            win = rd[HALO + r - SUBLANES:HALO + r + EPI_ROWS + SUBLANES, :]
            up = pltpu.roll(win, 1, 0)[SUBLANES:SUBLANES + EPI_ROWS]
            dn = pltpu.roll(win, rows - 1, 0)[SUBLANES:SUBLANES + EPI_ROWS]
            return cw[0:1] * up + cw[1:2] * win[SUBLANES:SUBLANES + EPI_ROWS] + cw[2:3] * dn + cb_ref[...]

        for r in range(0, tm, EPI_ROWS):
            a = conv(rd_a, cwa_ref, cba_ref, r)
            val = conv(rd_v, cwv_ref, cbv_ref, r)
            o_ref[r:r + EPI_ROWS, :] = (0.5 * a * (1.0 + lax.erf(a * math.sqrt(0.5))) * val).astype(BF16)

        xn = xn_ref[...]
        wr_a[...] = jnp.dot(xn, wa_ref[...], preferred_element_type=F32)
        wr_v[...] = jnp.dot(xn, wv_ref[...], preferred_element_type=F32)

        zero_row = jnp.zeros((1, wr_a.shape[1]), F32)

        @pl.when(at_start)
        def _():
            wr_a[HALO - 1:HALO, :] = zero_row
            wr_v[HALO - 1:HALO, :] = zero_row

        @pl.when(at_end)
        def _():
            wr_a[HALO + tm:HALO + tm + 1, :] = zero_row
            wr_v[HALO + tm:HALO + tm + 1, :] = zero_row

    @pl.when(s % 2 == 0)
    def _():
        step(ra0, rv0, ra1, rv1)

    @pl.when(s % 2 == 1)
    def _():
        step(ra1, rv1, ra0, rv0)


def _ffn_up(x, gain, w, cw, cb, *, seq):
    t, d = x.shape
    ff = w.shape[1] // 2
    tm, tn = _tile(seq, 1024), _tile(ff, 512)
    nj = ff // tn
    n_tiles = (t // tm) * nj
    slab = min(NORM_SLAB, tm)
    cb2 = cb.reshape(1, 2 * ff)

    def mm_tile(s):
        return jnp.minimum(s, n_tiles - 1) % nj

    def ep_tile(s):
        return jnp.maximum(s - 1, 0)

    raw = pltpu.VMEM((tm + 2 * HALO, tn), F32)
    return pl.pallas_call(
        functools.partial(_ffn_up_kernel, tm=tm, nj=nj, n_tiles=n_tiles, seq=seq, t_rows=t),
        grid=(n_tiles + 1,),
        in_specs=[
            pl.BlockSpec(memory_space=pl.ANY),
            pl.BlockSpec((1, d), lambda s: (0, 0)),
            pl.BlockSpec((d, tn), lambda s: (0, mm_tile(s))),
            pl.BlockSpec((d, tn), lambda s: (0, nj + mm_tile(s))),
            pl.BlockSpec((CONV_W, tn), lambda s: (0, ep_tile(s) % nj)),
            pl.BlockSpec((CONV_W, tn), lambda s: (0, nj + ep_tile(s) % nj)),
            pl.BlockSpec((1, tn), lambda s: (0, ep_tile(s) % nj)),
            pl.BlockSpec((1, tn), lambda s: (0, nj + ep_tile(s) % nj)),
        ],
        out_specs=pl.BlockSpec((tm, tn), lambda s: (ep_tile(s) // nj, ep_tile(s) % nj)),
        out_shape=jax.ShapeDtypeStruct((t, ff), BF16),
        scratch_shapes=[pltpu.VMEM((tm + 2 * HALO, d), BF16), pltpu.VMEM((2, slab, d), F32),
                        pltpu.VMEM((2, HALO, d), F32), pltpu.SemaphoreType.DMA((2,)),
                        pltpu.SemaphoreType.DMA((2,)), raw, raw, raw, raw],
        compiler_params=_params("arbitrary"),
        name="ffn_up",
    )(x, gain.reshape(1, d), w, w, cw, cw, cb2, cb2)


def _mlstm_kernel(big_ref, bfg_ref, q_ref, k_ref, v_ref, gi_ref, gf_ref, o_ref, ct_ref, m_ref,
                  *, heads, hps, dqk, dv, chunk):
    d = pl.program_id(0)
    hg = pl.program_id(2)
    c = pl.program_id(3)

    @pl.when(c == 0)
    def _():
        ct_ref[...] = jnp.zeros_like(ct_ref)
        m_ref[...] = jnp.zeros_like(m_ref)

    sign = 1 - 2 * d
    row = lax.broadcasted_iota(jnp.int32, (chunk, chunk), 0)
    col = lax.broadcasted_iota(jnp.int32, (chunk, chunk), 1)
    before = ((row - col) * sign <= 0).astype(BF16)
    mask = ((col - row) * sign <= 0)
    diag = row == col
    prow = lax.broadcasted_iota(jnp.int32, (BF16_SUBLANES, chunk), 0)
    ones_col = (lax.broadcasted_iota(jnp.int32, (chunk, LANES), 1) == 0).astype(BF16)
    kscale = dqk ** -0.5

    for u in range(hps):
        h = hg * hps + u
        gidx = d * heads + h
        i_row = gi_ref[pl.ds(h, 1), :] + big_ref[gidx]
        xf = gf_ref[pl.ds(h, 1), :] + bfg_ref[gidx]
        lf_row = jnp.minimum(xf, 0.0) - jnp.log1p(jnp.exp(-jnp.abs(xf)))

        hi = lf_row.astype(BF16).astype(F32)
        r1 = lf_row - hi
        mid = r1.astype(BF16).astype(F32)
        lo = r1 - mid
        parts = jnp.where(prow == 0, hi, jnp.where(prow == 1, mid, jnp.where(prow == 2, lo, 0.0)))
        b3 = jnp.dot(parts.astype(BF16), before, preferred_element_type=F32)
        b_row = b3[0:1] + b3[1:2] + b3[2:3]
        g_tot = jnp.sum(lf_row, axis=1, keepdims=True)

        r_row = i_row - b_row
        m_prev = m_ref[u, 0:1, 0:1]
        rm = jnp.where(mask, r_row, -jnp.inf)
        m_col = jnp.maximum(jnp.max(rm, axis=1, keepdims=True), m_prev)
        m_all = jnp.maximum(jnp.max(r_row, axis=1, keepdims=True), m_prev)
        dmat = jnp.exp(rm - m_col)

        q = q_ref[:, u * dqk:(u + 1) * dqk]
        k = k_ref[:, u * dqk:(u + 1) * dqk]
        v_aug = jnp.concatenate([v_ref[:, u * dv:(u + 1) * dv], ones_col], axis=1)

        qk = lax.dot_general(q, k, (((1,), (1,)), ((), ())), preferred_element_type=F32)
        s = (qk * kscale * dmat).astype(BF16)
        ct = ct_ref[u]
        inter = jnp.dot(q, ct.astype(BF16), preferred_element_type=F32)
        sc = jnp.exp(m_prev - m_col) * kscale
        numden = sc * inter + jnp.dot(s, v_aug, preferred_element_type=F32)
        num = numden[:, :dv]
        den = numden[:, dv:dv + 1]
        b_col = jnp.sum(jnp.where(diag, b_row, 0.0), axis=1, keepdims=True)
        floor = jnp.exp(-(b_col + m_col))
        o_ref[:, u * dv:(u + 1) * dv] = (num * (1.0 / jnp.maximum(jnp.abs(den), floor))).astype(BF16)

        wa_row = jnp.exp(r_row - m_all)
        decay = jnp.exp(m_prev - m_all)
        ktw = (k.astype(F32).T * wa_row).astype(BF16)
        ct_ref[u] = decay * ct + jnp.dot(ktw, v_aug, preferred_element_type=F32)
        m_ref[u] = jnp.broadcast_to(g_tot + m_all, m_ref.shape[1:])


def _mlstm(p, gates_t, b_ig, b_fg, *, batch, seq, heads, dqk, dv):
    t = batch * seq
    chunk = _tile(seq, M_CHUNK)
    nc = seq // chunk
    hps = M_HEADS_PER_STEP
    ng = heads // hps
    koff = ng
    voff = 2 * heads * dqk // (hps * dv)

    def rows(d, b, c):
        return b * nc + c + d * (nc - 1 - 2 * c)

    smem = pl.BlockSpec(memory_space=pltpu.SMEM)
    return pl.pallas_call(
        functools.partial(_mlstm_kernel, heads=heads, hps=hps, dqk=dqk, dv=dv, chunk=chunk),
        grid=(2, batch, ng, nc),
        in_specs=[
            smem, smem,
            pl.BlockSpec((chunk, hps * dqk), lambda d, b, g, c: (rows(d, b, c), g)),
            pl.BlockSpec((chunk, hps * dqk), lambda d, b, g, c: (rows(d, b, c), koff + g)),
            pl.BlockSpec((chunk, hps * dv), lambda d, b, g, c: (rows(d, b, c), voff + g)),
            pl.BlockSpec((heads, chunk), lambda d, b, g, c: (d, rows(d, b, c))),
            pl.BlockSpec((heads, chunk), lambda d, b, g, c: (2 + d, rows(d, b, c))),
        ],
        out_specs=pl.BlockSpec((None, chunk, hps * dv), lambda d, b, g, c: (d, rows(d, b, c), g)),
        out_shape=jax.ShapeDtypeStruct((2, t, heads * dv), BF16),
        scratch_shapes=[pltpu.VMEM((hps, dqk, dv + LANES), F32), pltpu.VMEM((hps, 8, LANES), F32)],
        compiler_params=_params("parallel", "parallel", "parallel", "arbitrary"),
        name="mlstm",
    )(b_ig, b_fg, p, p, p, gates_t, gates_t)


def _head_norm_kernel(h_ref, og_ref, g_ref, o_ref, *, dv):
    for c0 in range(0, o_ref.shape[1], dv):
        cols = slice(c0, c0 + dv)
        hs = h_ref[0, :, cols].astype(F32) + h_ref[1, :, cols].astype(F32)
        ms = jnp.mean(hs * hs, axis=-1, keepdims=True)
        y = hs * lax.rsqrt(ms + EPS) * g_ref[:, cols]
        o_ref[:, cols] = (y * jax.nn.sigmoid(og_ref[:, cols].astype(F32))).astype(BF16)


def _head_norm(h2, p, gain, *, heads, dv, ooff):
    _, t, vw = h2.shape
    tm = _tile(t, 1024)
    wd = 2 * dv
    return pl.pallas_call(
        functools.partial(_head_norm_kernel, dv=dv),
        grid=(t // tm, vw // wd),
        in_specs=[
            pl.BlockSpec((2, tm, wd), lambda i, h: (0, i, h)),
            pl.BlockSpec((tm, wd), lambda i, h: (i, ooff // wd + h)),
            pl.BlockSpec((1, wd), lambda i, h: (0, h)),
        ],
        out_specs=pl.BlockSpec((tm, wd), lambda i, h: (i, h)),
        out_shape=jax.ShapeDtypeStruct((t, vw), BF16),
        compiler_params=_params("parallel", "parallel"),
        name="head_norm",
    )(h2, p, gain.reshape(1, vw))


def _twiddles(n, cols):
    r = math.gcd(n, 64)
    k = jnp.arange(cols, dtype=jnp.int32)[None, :]

    def cos_sin(j):
        ang = ((j[:, None] * k) % n).astype(F32) * (2.0 * math.pi / n)
        return jnp.cos(ang), jnp.sin(ang)

    ac, asn = cos_sin(jnp.arange(n // r, dtype=jnp.int32) * r)
    bc, bsn = cos_sin(jnp.arange(r, dtype=jnp.int32))
    cos = (ac[:, None, :] * bc[None] - asn[:, None, :] * bsn[None]).reshape(n, cols)
    sin = (asn[:, None, :] * bc[None] + ac[:, None, :] * bsn[None]).reshape(n, cols)
    return cos, sin


def _dft_fold_kernel(xd_ref, xa_ref, xb_ref, w_ref, o_ref, *, tm, gdim):
    u = lax.broadcasted_iota(jnp.int32, (tm, 2 * tm), 0)
    v = lax.broadcasted_iota(jnp.int32, (tm, 2 * tm), 1)
    pick = (v == tm - u).astype(BF16)
    pair = jnp.concatenate([xa_ref[...], xb_ref[...]], axis=0)
    xr = jnp.dot(pick, pair, preferred_element_type=F32)
    xd = xd_ref[...].astype(F32)
    w = w_ref[...]
    o_ref[:, :gdim] = jnp.dot((xd + xr).astype(BF16), w[:, :gdim], preferred_element_type=F32).astype(BF16)
    o_ref[:, gdim:] = jnp.dot((xd - xr).astype(BF16), w[:, gdim:], preferred_element_type=F32).astype(BF16)


def _dft_fold(p, w_cs, *, batch, seq, groups, gdim, froff):
    half = seq // 2
    tm = _tile(half, 256)
    nt, nh = seq // tm, half // tm
    cb = froff // gdim
    return pl.pallas_call(
        functools.partial(_dft_fold_kernel, tm=tm, gdim=gdim),
        grid=(batch, nh, groups),
        in_specs=[
            pl.BlockSpec((tm, gdim), lambda b, i, g: (b * nt + i, cb + g)),
            pl.BlockSpec((tm, gdim), lambda b, i, g: (b * nt + nt - 1 - i, cb + g)),
            pl.BlockSpec((tm, gdim), lambda b, i, g: (b * nt + (nt - i) % nt, cb + g)),
            pl.BlockSpec((gdim, 2 * gdim), lambda b, i, g: (0, 0)),
        ],
        out_specs=pl.BlockSpec((tm, 2 * gdim), lambda b, i, g: (b * nh + i, g)),
        out_shape=jax.ShapeDtypeStruct((batch * half, 2 * groups * gdim), BF16),
        compiler_params=_params("parallel", "parallel", "parallel"),
        name="dft_fold",
    )(p, p, p, w_cs)


def _dft_pos_kernel(c_ref, s_ref, e_ref, om_ref, xh_ref, wc_ref, o_ref, yh_ref, *, scale, tk):
    i = pl.program_id(2)

    @pl.when(i == 0)
    def _():
        yh_ref[...] = jnp.dot(xh_ref[...], wc_ref[...], preferred_element_type=F32)

    acc = jnp.dot(c_ref[...], e_ref[...], preferred_element_type=F32)
    acc = acc + jnp.dot(s_ref[...], om_ref[...], preferred_element_type=F32)
    k1 = i * tk + lax.broadcasted_iota(jnp.int32, (tk, 1), 0)
    sgn = (1 - 2 * (k1 & 1)).astype(F32)
    o_ref[...] = ((acc + sgn * yh_ref[0:1, :]) * scale).astype(BF16)


def _dft_pos(y, p, cos_w, nsin, w_cs, *, batch, seq, groups, gdim, froff):
    t = batch * seq
    half = seq // 2
    tk = _tile(seq, 1024)
    nk = seq // tk
    cb = froff // gdim
    scale = 1.0 / math.sqrt(seq * gdim)
    y_buffers = 2 if half * gdim <= DFT_DOUBLE_BUFFER_ELEMS else 1
    return pl.pallas_call(
        functools.partial(_dft_pos_kernel, scale=scale, tk=tk),
        grid=(batch, groups, nk),
        in_specs=[
            pl.BlockSpec((tk, half), lambda b, g, i: (i, 0)),
            pl.BlockSpec((tk, half), lambda b, g, i: (i, 0)),
            pl.BlockSpec((half, gdim), lambda b, g, i: (b, 2 * g), pipeline_mode=pl.Buffered(y_buffers)),
            pl.BlockSpec((half, gdim), lambda b, g, i: (b, 2 * g + 1), pipeline_mode=pl.Buffered(y_buffers)),
            pl.BlockSpec((BF16_SUBLANES, gdim),
                         lambda b, g, i: ((b * seq + half) // BF16_SUBLANES, cb + g)),
            pl.BlockSpec((gdim, gdim), lambda b, g, i: (0, 0)),
        ],
        out_specs=pl.BlockSpec((tk, gdim), lambda b, g, i: (b * nk + i, g)),
        out_shape=jax.ShapeDtypeStruct((t, groups * gdim), BF16),
        scratch_shapes=[pltpu.VMEM((BF16_SUBLANES, gdim), F32)],
        compiler_params=_params("parallel", "parallel", "arbitrary"),
        name="dft_pos",
    )(cos_w, nsin, y, y, p, w_cs)


def _merge_kernel(hn_ref, fr_ref, wmo_ref, wfo_ref, gm_ref, gf_ref, bm_ref, bf_ref, o_ref):
    hm = jnp.dot(hn_ref[...], wmo_ref[...], preferred_element_type=F32)
    hf = jnp.dot(fr_ref[...], wfo_ref[...], preferred_element_type=F32)
    g1 = jax.nn.sigmoid(gm_ref[...].astype(F32) + bm_ref[...])
    g2 = jax.nn.sigmoid(gf_ref[...].astype(F32) + bf_ref[...])
    o_ref[...] = (g1 * hm + g2 * hf).astype(BF16)


def _merge(hn, fr, w_mo, w_fo, p, b_merge, *, gmoff):
    t, vw = hn.shape
    fw = fr.shape[1]
    d = w_mo.shape[1]
    tm, tn = _tile(t, 1024), _tile(d, 512)
    nj = d // tn
    bm = b_merge.reshape(1, 2 * d)
    return pl.pallas_call(
        _merge_kernel,
        grid=(t // tm, nj),
        in_specs=[
            pl.BlockSpec((tm, vw), lambda i, j: (i, 0), pipeline_mode=pl.Buffered(1)),
            pl.BlockSpec((tm, fw), lambda i, j: (i, 0), pipeline_mode=pl.Buffered(1)),
            pl.BlockSpec((vw, tn), lambda i, j: (0, j)),
            pl.BlockSpec((fw, tn), lambda i, j: (0, j)),
            pl.BlockSpec((tm, tn), lambda i, j: (i, gmoff // tn + j)),
            pl.BlockSpec((tm, tn), lambda i, j: (i, gmoff // tn + nj + j)),
            pl.BlockSpec((1, tn), lambda i, j: (0, j)),
            pl.BlockSpec((1, tn), lambda i, j: (0, nj + j)),
        ],
        out_specs=pl.BlockSpec((tm, tn), lambda i, j: (i, j)),
        out_shape=jax.ShapeDtypeStruct((t, d), BF16),
        compiler_params=_params("parallel", "arbitrary"),
        name="merge",
    )(hn, fr, w_mo, w_fo, p, p, bm, bm)


def _proj_res_kernel(a_ref, w_ref, r_ref, o_ref):
    o_ref[...] = r_ref[...] + jnp.dot(a_ref[...], w_ref[...], preferred_element_type=F32)


def _proj_res(a, w, res, *, tn, a_buffers, name):
    t, kdim = a.shape
    n = w.shape[1]
    tm, tn = _tile(t, 1024), _tile(n, tn)
    return pl.pallas_call(
        _proj_res_kernel,
        grid=(t // tm, n // tn),
        in_specs=[
            pl.BlockSpec((tm, kdim), lambda i, j: (i, 0), pipeline_mode=pl.Buffered(a_buffers)),
            pl.BlockSpec((kdim, tn), lambda i, j: (0, j)),
            pl.BlockSpec((tm, tn), lambda i, j: (i, j)),
        ],
        out_specs=pl.BlockSpec((tm, tn), lambda i, j: (i, j)),
        out_shape=jax.ShapeDtypeStruct((t, n), F32),
        compiler_params=_params("parallel", "arbitrary"),
        name=name,
    )(a, w, res)


def _final_norm_kernel(x_ref, g_ref, o_ref):
    x = x_ref[...]
    ms = jnp.mean(x * x, axis=-1, keepdims=True)
    o_ref[...] = x * lax.rsqrt(ms + EPS) * g_ref[...]


def _final_norm(x, gain):
    t, d = x.shape
    tm = _tile(t, 256)
    return pl.pallas_call(
        _final_norm_kernel,
        grid=(t // tm,),
        in_specs=[pl.BlockSpec((tm, d), lambda i: (i, 0)), pl.BlockSpec((1, d), lambda i: (0, 0))],
        out_specs=pl.BlockSpec((tm, d), lambda i: (i, 0)),
        out_shape=jax.ShapeDtypeStruct((t, d), F32),
        compiler_params=_params("parallel"),
        name="final_norm",
    )(x, gain.reshape(1, d))


def _cast_skip_kernel(a_ref, b_ref, o_ref, *, first_shifted, shift):
    a = a_ref[...]
    shifted = jnp.concatenate([a[:, shift:], b_ref[:, :shift]], axis=1)
    o_ref[...] = jnp.where(pl.program_id(1) >= first_shifted, shifted, a).astype(BF16)


def _cast_skip_cols(w, skip_at, skip):
    _, rows, cols = w.shape
    n_out = cols - skip
    tc = next(c for c in (1024, 512, 256, LANES) if skip_at % c == 0 and n_out % c == 0)
    tr = _tile(rows, 1024)
    per = tc // LANES
    return pl.pallas_call(
        functools.partial(_cast_skip_kernel, first_shifted=skip_at // tc, shift=skip),
        grid=(rows // tr, n_out // tc),
        in_specs=[pl.BlockSpec((None, tr, tc), lambda i, j: (0, i, j)),
                  pl.BlockSpec((None, tr, LANES), lambda i, j: (0, i, (j + 1) * per))],
        out_specs=pl.BlockSpec((tr, tc), lambda i, j: (i, j)),
        out_shape=jax.ShapeDtypeStruct((rows, n_out), BF16),
        compiler_params=_params("parallel", "parallel"),
        name="cast_skip_cols",
    )(w, w)


def _cast_pad_kernel(a_ref, o_ref, *, axis, real_tiles, period):
    is_pad = (pl.program_id(axis) % period) >= real_tiles
    o_ref[...] = jnp.where(is_pad, 0.0, a_ref[...]).astype(BF16)


def _cast_pad_halves(w, ff, ffp, *, axis):
    tile = 256
    dims = w.shape[1:]
    assert ff % tile == 0 and ffp % tile == 0 and dims[axis] % ff == 0
    real, period = ff // tile, ffp // tile
    sections = dims[axis] // ff
    other = dims[1 - axis]
    to = _tile(other, 4096)

    def src(t):
        return (t // period) * real + jnp.minimum(t % period, real - 1)

    if axis == 1:
        block, grid = (to, tile), (other // to, sections * period)
        in_map = lambda i, j: (0, i, src(j))
        out_shape = (other, sections * ffp)
    else:
        block, grid = (tile, to), (sections * period, other // to)
        in_map = lambda i, j: (0, src(i), j)
        out_shape = (sections * ffp, other)
    return pl.pallas_call(
        functools.partial(_cast_pad_kernel, axis=axis, real_tiles=real, period=period),
        grid=grid,
        in_specs=[pl.BlockSpec((None,) + block, in_map)],
        out_specs=pl.BlockSpec(block, lambda i, j: (i, j)),
        out_shape=jax.ShapeDtypeStruct(out_shape, BF16),
        compiler_params=_params("parallel", "parallel"),
        name="cast_pad",
    )(w)


def _pad_cols(a, n):
    return jnp.pad(a, ((0, 0), (0, n - a.shape[1])))


def _prep_weights(w_in, b_igate, b_fgate, w_mlstm_out, w_fourier_out, w_out, w_up, conv_w, conv_b, w_down,
                  *, heads, vw, fw):
    d, in_w = w_in.shape[1:]
    qk_w = (in_w - 2 * vw - 4 * heads - fw - 2 * d) // 2
    off_i = 2 * qk_w + 2 * vw
    off_fr = off_i + 4 * heads
    w_main = _cast_skip_cols(w_in, off_i, off_fr - off_i)
    w_gate = jnp.pad(w_in[0, :, off_i:off_fr].T, ((0, GATE_ROWS - 4 * heads), (0, 0))).astype(BF16)
    ff = w_down.shape[1]
    ffp = -(-ff // 1024) * 1024
    w_up_p = _cast_pad_halves(w_up, ff, ffp, axis=1)
    cw_p = jnp.concatenate([_pad_cols(conv_w[0][:, :ff], ffp), _pad_cols(conv_w[0][:, ff:], ffp)], axis=1)
    cb_p = jnp.concatenate([jnp.pad(conv_b[0][:ff], (0, ffp - ff)), jnp.pad(conv_b[0][ff:], (0, ffp - ff))])
    w_down_p = _cast_pad_halves(w_down, ff, ffp, axis=0)
    return dict(
        w_main=w_main, w_gate=w_gate, qk_w=qk_w,
        b_ig=b_igate[0].reshape(-1), b_fg=b_fgate[0].reshape(-1),
        w_mo=w_mlstm_out[0].astype(BF16), w_fo=w_fourier_out[0].astype(BF16), w_out=w_out[0].astype(BF16),
        w_up=w_up_p, cw=cw_p, cb=cb_p, w_down=w_down_p, ffp=ffp,
    )


def _trunk(x, wts, norm_mix, mh_norm, b_merge, norm_ffn, norm_final, *, heads, groups):
    batch, seq, d = x.shape
    t = batch * seq
    x2 = x.reshape(t, d)
    vw = wts["w_mo"].shape[0]
    fw = wts["w_fo"].shape[0]
    qk_w = wts["qk_w"]
    dqk, dv, gdim = qk_w // heads, vw // heads, fw // groups
    ooff = 2 * qk_w + vw
    froff = ooff + vw
    gmoff = froff + fw

    p, gates_t = _in_proj(x2, norm_mix[0], wts["w_main"], wts["w_gate"])
    h2 = _mlstm(p, gates_t, wts["b_ig"], wts["b_fg"], batch=batch, seq=seq, heads=heads, dqk=dqk, dv=dv)
    hn = _head_norm(h2, p, mh_norm[0], heads=heads, dv=dv, ooff=ooff)

    cos_c, sin_c = _twiddles(gdim, gdim)
    w_cs = jnp.concatenate([cos_c, sin_c], axis=1).astype(BF16)
    cos_s, sin_s = _twiddles(seq, seq // 2)
    col = lax.broadcasted_iota(jnp.int32, cos_s.shape, 1)
    cos_w = jnp.where(col == 0, 0.5, cos_s).astype(BF16)
    y = _dft_fold(p, w_cs, batch=batch, seq=seq, groups=groups, gdim=gdim, froff=froff)
    fr = _dft_pos(y, p, cos_w, (-sin_s).astype(BF16), w_cs,
                  batch=batch, seq=seq, groups=groups, gdim=gdim, froff=froff)

    merged = _merge(hn, fr, wts["w_mo"], wts["w_fo"], p, b_merge[0], gmoff=gmoff)
    x1 = _proj_res(merged, wts["w_out"], x2, tn=512, a_buffers=2, name="out_proj")

    hf = _ffn_up(x1, norm_ffn[0], wts["w_up"], wts["cw"], wts["cb"], seq=seq)
    ffp = wts["ffp"]
    x3 = _proj_res(hf, wts["w_down"], x1, tn=256, a_buffers=1, name="down_proj")
    return _final_norm(x3, norm_final).reshape(batch, seq, d)


def kernel(x_prompt, x_sample, norm_mix, w_in, b_igate, b_fgate, mh_norm, w_mlstm_out, w_fourier_out, b_merge, w_out, norm_ffn, w_up, conv_w, conv_b, w_down, norm_final):
    heads = b_igate.shape[-1]
    vw, fw = w_mlstm_out.shape[1], w_fourier_out.shape[1]
    wts = _prep_weights(w_in, b_igate, b_fgate, w_mlstm_out, w_fourier_out, w_out, w_up, conv_w, conv_b, w_down,
                        heads=heads, vw=vw, fw=fw)
    run = functools.partial(_trunk, wts=wts, norm_mix=norm_mix, mh_norm=mh_norm, b_merge=b_merge,
                            norm_ffn=norm_ffn, norm_final=norm_final, heads=heads, groups=F_GROUPS)
    return (run(x_prompt), run(x_sample))
```
